```python
import math
import jax, jax.numpy as jnp
from jax import lax
import numpy as np

D_MODEL = 2048
BATCH = 2
SEQ = 8192
DEPTH = 1
DEC_BATCH = 32
DEC_SEQ = 4
PAST_LEN = 16384
PAGE_SIZE = 128

N_HEADS = 16
HEAD_DIM = D_MODEL // N_HEADS
D_ATTN = N_HEADS * HEAD_DIM
D_CONV = D_MODEL
CONV_WIDTH = 31
D_FF = 4 * D_MODEL
MOBA_BLOCK = 256
MOBA_TOPK = 3
Q_CHUNK = 64
LN_EPS = 1e-5
DEEPNORM_ALPHA = (2.0 * DEPTH) ** 0.25
DEEPNORM_BETA = (8.0 * DEPTH) ** -0.25
D_IN = 3 * D_ATTN + 2 * D_CONV + 2 * D_MODEL

kernel_name = 'moba_conformer_gated_hybrid_step'


def _layer_norm(x, g, b):
    xf = x.astype(jnp.float32)
    mu = jnp.mean(xf, axis=-1, keepdims=True)
    var = jnp.mean(jnp.square(xf - mu), axis=-1, keepdims=True)
    return ((xf - mu) * lax.rsqrt(var + LN_EPS) * g + b).astype(x.dtype)


def _alibi_slopes():
    return jnp.asarray(2.0 ** (-8.0 * np.arange(1, N_HEADS + 1) / N_HEADS), jnp.float32)


def _in_proj(x, w_in, b_in):
    z = jnp.einsum('bsd,de->bse', x, w_in) + b_in
    cuts = [D_ATTN, 2 * D_ATTN, 3 * D_ATTN, 3 * D_ATTN + D_CONV, 3 * D_ATTN + 2 * D_CONV,
            3 * D_ATTN + 2 * D_CONV + D_MODEL]
    q, k, v, a, g, ga, gb = jnp.split(z, cuts, axis=-1)
    heads = lambda t: t.reshape(t.shape[0], t.shape[1], N_HEADS, HEAD_DIM)
    return heads(q), heads(k), heads(v), a * jax.nn.sigmoid(g), jax.nn.sigmoid(ga), jax.nn.sigmoid(gb)


def _conv_branch(u_ext, w_dw, b_dw, ln_g, ln_b, w_o):
    c = lax.conv_general_dilated(u_ext, w_dw[:, None, :], window_strides=(1,), padding='VALID',
                                 dimension_numbers=('NWC', 'WIO', 'NWC'),
                                 feature_group_count=D_CONV) + b_dw
    h = jax.nn.silu(_layer_norm(c, ln_g, ln_b))
    return jnp.einsum('bsc,cd->bsd', h, w_o)


def _merge_and_mlp(x, o_attn, o_conv, gate_a, gate_b, w_attn_o, w_out, ln1_g, ln1_b,
                   w_mlp1, w_mlp2, ln2_g, ln2_b):
    br_attn = jnp.einsum('bse,ed->bsd', o_attn, w_attn_o)
    mixed = gate_a * br_attn + gate_b * o_conv
    h = _layer_norm(DEEPNORM_ALPHA * x + jnp.einsum('bsd,de->bse', mixed, w_out), ln1_g, ln1_b)
    f = jnp.square(jax.nn.relu(jnp.einsum('bsd,df->bsf', h, w_mlp1)))
    f = jnp.einsum('bsf,fd->bsd', f, w_mlp2)
    return _layer_norm(DEEPNORM_ALPHA * h + f, ln2_g, ln2_b)


def _moba_prompt(q, k, v):
    B, S = q.shape[0], q.shape[1]
    N = B * N_HEADS
    n_blk = -(-S // MOBA_BLOCK)
    s_pad = n_blk * MOBA_BLOCK
    n_sel = min(MOBA_TOPK, n_blk - 1)
    scale = HEAD_DIM ** -0.5
    to_nsd = lambda a: a.transpose(0, 2, 1, 3).reshape(N, S, HEAD_DIM)
    qn, kn, vn = to_nsd(q), to_nsd(k), to_nsd(v)
    pad = ((0, 0), (0, s_pad - S), (0, 0))
    kp, vp = jnp.pad(kn, pad), jnp.pad(vn, pad)
    kb = kp.reshape(N, n_blk, MOBA_BLOCK, HEAD_DIM)
    vb = vp.reshape(N, n_blk, MOBA_BLOCK, HEAD_DIM)
    kbar = jnp.mean(kb, axis=2, dtype=jnp.float32)
    slope = jnp.tile(_alibi_slopes(), B)[:, None, None]

    def chunk(c):
        start = c * Q_CHUNK
        t = start + jnp.arange(Q_CHUNK)
        blk = start // MOBA_BLOCK
        qc = lax.dynamic_slice_in_dim(qn, start, Q_CHUNK, axis=1)
        k_own = lax.dynamic_slice_in_dim(kp, blk * MOBA_BLOCK, MOBA_BLOCK, axis=1)
        v_own = lax.dynamic_slice_in_dim(vp, blk * MOBA_BLOCK, MOBA_BLOCK, axis=1)
        s_own = blk * MOBA_BLOCK + jnp.arange(MOBA_BLOCK)
        dist_own = (t[:, None] - s_own[None, :]).astype(jnp.float32)
        l_own = jnp.einsum('nqd,nsd->nqs', qc, k_own).astype(jnp.float32) * scale - slope * dist_own
        l_own = jnp.where((s_own[None, :] <= t[:, None])[None], l_own, -jnp.inf)
        if n_sel > 0:
            bs = jnp.einsum('nqd,njd->nqj', qc.astype(jnp.float32), kbar)
            bs = jnp.where(jnp.arange(n_blk) < blk, bs, -jnp.inf)
            _, idx = lax.top_k(bs, n_sel)
            k_sel = jax.vmap(lambda a, i: a[i])(kb, idx)
            v_sel = jax.vmap(lambda a, i: a[i])(vb, idx)
            s_sel = idx[..., None] * MOBA_BLOCK + jnp.arange(MOBA_BLOCK)
            dist_sel = (t[None, :, None, None] - s_sel).astype(jnp.float32)
            l_sel = jnp.einsum('nqd,nqrsd->nqrs', qc, k_sel).astype(jnp.float32) * scale \
                - slope[..., None] * dist_sel
            l_sel = jnp.where((idx < blk)[..., None], l_sel, -jnp.inf)
            l_sel = l_sel.reshape(N, Q_CHUNK, n_sel * MOBA_BLOCK)
            p = jax.nn.softmax(jnp.concatenate([l_sel, l_own], axis=-1), axis=-1)
            p_sel = p[..., :n_sel * MOBA_BLOCK].reshape(N, Q_CHUNK, n_sel, MOBA_BLOCK)
            p_own = p[..., n_sel * MOBA_BLOCK:]
            o = jnp.einsum('nqrs,nqrsd->nqd', p_sel, v_sel) + jnp.einsum('nqs,nsd->nqd', p_own, v_own)
        else:
            p_own = jax.nn.softmax(l_own, axis=-1)
            o = jnp.einsum('nqs,nsd->nqd', p_own, v_own)
        return o.astype(q.dtype)

    out = lax.map(chunk, jnp.arange(S // Q_CHUNK))
    out = out.transpose(1, 0, 2, 3).reshape(B, N_HEADS, S, HEAD_DIM)
    return out.transpose(0, 2, 1, 3).reshape(B, S, D_ATTN)


def _moba_sample(q, k, v, cache_k, cache_v, page_table):
    DB, T = q.shape[0], q.shape[1]
    n_pages = PAST_LEN // PAGE_SIZE
    ppb = MOBA_BLOCK // PAGE_SIZE
    n_past_blk = PAST_LEN // MOBA_BLOCK
    own_start = n_past_blk * MOBA_BLOCK
    n_own_past = PAST_LEN - own_start
    n_sel = min(MOBA_TOPK, n_past_blk)
    scale = HEAD_DIM ** -0.5
    slope = _alibi_slopes()[None, :, None, None]
    t = PAST_LEN + jnp.arange(T)
    own_pages = page_table[:, n_pages - n_own_past // PAGE_SIZE:]
    k_own = jnp.concatenate([cache_k[own_pages].reshape(DB, n_own_past, N_HEADS, HEAD_DIM), k], axis=1)
    v_own = jnp.concatenate([cache_v[own_pages].reshape(DB, n_own_past, N_HEADS, HEAD_DIM), v], axis=1)
    s_own = own_start + jnp.arange(n_own_past + T)
    dist_own = (t[:, None] - s_own[None, :]).astype(jnp.float32)
    l_own = jnp.einsum('bthd,bshd->bhts', q, k_own).astype(jnp.float32) * scale - slope * dist_own
    l_own = jnp.where((s_own[None, :] <= t[:, None])[None, None], l_own, -jnp.inf)
    if n_sel > 0:
        page_sum = jnp.sum(cache_k, axis=1, dtype=jnp.float32)
        kbar = page_sum[page_table[:, :n_past_blk * ppb]].reshape(
            DB, n_past_blk, ppb, N_HEADS, HEAD_DIM).sum(axis=2) / MOBA_BLOCK
        bs = jnp.einsum('bthd,bjhd->bhtj', q.astype(jnp.float32), kbar)
        _, idx = lax.top_k(bs, n_sel)
        logical = idx[..., None] * ppb + jnp.arange(ppb)
        phys = page_table[jnp.arange(DB)[:, None, None, None, None], logical]
        hh = jnp.arange(N_HEADS)[None, :, None, None, None]
        k_sel = cache_k[phys, :, hh, :].reshape(DB, N_HEADS, T, n_sel * MOBA_BLOCK, HEAD_DIM)
        v_sel = cache_v[phys, :, hh, :].reshape(DB, N_HEADS, T, n_sel * MOBA_BLOCK, HEAD_DIM)
        s_sel = (idx[..., None] * MOBA_BLOCK + jnp.arange(MOBA_BLOCK)).reshape(
            DB, N_HEADS, T, n_sel * MOBA_BLOCK)
        dist_sel = (t[None, None, :, None] - s_sel).astype(jnp.float32)
        qh = q.transpose(0, 2, 1, 3)
        l_sel = jnp.einsum('bhtd,bhtsd->bhts', qh, k_sel).astype(jnp.float32) * scale - slope * dist_sel
        p = jax.nn.softmax(jnp.concatenate([l_sel, l_own], axis=-1), axis=-1)
        p_sel, p_own = p[..., :n_sel * MOBA_BLOCK], p[..., n_sel * MOBA_BLOCK:]
        o = jnp.einsum('bhts,bhtsd->bthd', p_sel, v_sel) + jnp.einsum('bhts,bshd->bthd', p_own, v_own)
    else:
        p_own = jax.nn.softmax(l_own, axis=-1)
        o = jnp.einsum('bhts,bshd->bthd', p_own, v_own)
    return o.astype(q.dtype).reshape(DB, T, D_ATTN)


def setup_inputs(seed: int = 0) -> dict:
    key = jax.random.key(seed)
    ks = jax.random.split(key, 24)
    f32 = jnp.float32
    n_pages = PAST_LEN // PAGE_SIZE
    n_used = DEC_BATCH * n_pages
    n_phys = n_used + n_used // 4
    nrm = lambda k, shape, s: jax.random.normal(k, shape, f32) * s
    page_table = jax.random.permutation(ks[5], n_phys)[:n_used].reshape(DEC_BATCH, n_pages).astype(jnp.int32)
    return {
        'x_prompt': nrm(ks[0], (BATCH, SEQ, D_MODEL), 1.0),
        'x_sample': nrm(ks[1], (DEC_BATCH, DEC_SEQ, D_MODEL), 1.0),
        'cache_k': nrm(ks[2], (DEPTH, n_phys, PAGE_SIZE, N_HEADS, HEAD_DIM), 1.0),
        'cache_v': nrm(ks[3], (DEPTH, n_phys, PAGE_SIZE, N_HEADS, HEAD_DIM), 1.0),
        'state_conv': nrm(ks[4], (DEPTH, DEC_BATCH, CONV_WIDTH - 1, D_CONV), 0.5),
        'page_table': page_table,
        'w_in': nrm(ks[6], (DEPTH, D_MODEL, D_IN), D_MODEL ** -0.5),
        'b_in': nrm(ks[7], (DEPTH, D_IN), 0.01),
        'w_attn_o': nrm(ks[8], (DEPTH, D_ATTN, D_MODEL), D_ATTN ** -0.5 * DEEPNORM_BETA),
        'w_dw': nrm(ks[9], (DEPTH, CONV_WIDTH, D_CONV), CONV_WIDTH ** -0.5),
        'b_dw': nrm(ks[10], (DEPTH, D_CONV), 0.01),
        'ln_conv_g': 1.0 + nrm(ks[11], (DEPTH, D_CONV), 0.01),
        'ln_conv_b': nrm(ks[12], (DEPTH, D_CONV), 0.01),
        'w_conv_o': nrm(ks[13], (DEPTH, D_CONV, D_MODEL), D_CONV ** -0.5 * DEEPNORM_BETA),
        'w_out': nrm(ks[14], (DEPTH, D_MODEL, D_MODEL), D_MODEL ** -0.5 * DEEPNORM_BETA),
        'ln1_g': 1.0 + nrm(ks[15], (DEPTH, D_MODEL), 0.01),
        'ln1_b': nrm(ks[16], (DEPTH, D_MODEL), 0.01),
        'w_mlp1': nrm(ks[17], (DEPTH, D_MODEL, D_FF), D_MODEL ** -0.5),
        'w_mlp2': nrm(ks[18], (DEPTH, D_FF, D_MODEL), D_FF ** -0.5 * DEEPNORM_BETA),
        'ln2_g': 1.0 + nrm(ks[19], (DEPTH, D_MODEL), 0.01),
        'ln2_b': nrm(ks[20], (DEPTH, D_MODEL), 0.01),
    }


def reference(x_prompt, x_sample, cache_k, cache_v, state_conv, page_table, w_in, b_in, w_attn_o,
              w_dw, b_dw, ln_conv_g, ln_conv_b, w_conv_o, w_out, ln1_g, ln1_b, w_mlp1, w_mlp2,
              ln2_g, ln2_b):
    hp, hs = x_prompt, x_sample
    kp_l, vp_l, cp_l, ks_l, vs_l, cs_l = [], [], [], [], [], []
    for l in range(DEPTH):
        q, k, v, glu, ga, gb = _in_proj(hp, w_in[l], b_in[l])
        o_attn = _moba_prompt(q, k, v)
        u_ext = jnp.pad(glu, ((0, 0), (CONV_WIDTH - 1, 0), (0, 0)))
        o_conv = _conv_branch(u_ext, w_dw[l], b_dw[l], ln_conv_g[l], ln_conv_b[l], w_conv_o[l])
        kp_l.append(k.reshape(BATCH, SEQ // PAGE_SIZE, PAGE_SIZE, N_HEADS, HEAD_DIM))
        vp_l.append(v.reshape(BATCH, SEQ // PAGE_SIZE, PAGE_SIZE, N_HEADS, HEAD_DIM))
        cp_l.append(u_ext[:, -(CONV_WIDTH - 1):])
        hp = _merge_and_mlp(hp, o_attn, o_conv, ga, gb, w_attn_o[l], w_out[l], ln1_g[l], ln1_b[l],
                            w_mlp1[l], w_mlp2[l], ln2_g[l], ln2_b[l])
        q, k, v, glu, ga, gb = _in_proj(hs, w_in[l], b_in[l])
        o_attn = _moba_sample(q, k, v, cache_k[l], cache_v[l], page_table)
        u_ext = jnp.concatenate([state_conv[l].astype(glu.dtype), glu], axis=1)
        o_conv = _conv_branch(u_ext, w_dw[l], b_dw[l], ln_conv_g[l], ln_conv_b[l], w_conv_o[l])
        ks_l.append(k)
        vs_l.append(v)
        cs_l.append(u_ext[:, -(CONV_WIDTH - 1):])
        hs = _merge_and_mlp(hs, o_attn, o_conv, ga, gb, w_attn_o[l], w_out[l], ln1_g[l], ln1_b[l],
                            w_mlp1[l], w_mlp2[l], ln2_g[l], ln2_b[l])
    return (hp, hs, jnp.stack(kp_l), jnp.stack(vp_l), jnp.stack(cp_l),
            jnp.stack(ks_l), jnp.stack(vs_l), jnp.stack(cs_l))
```

```python
import functools

import numpy as np
import jax
import jax.numpy as jnp
from jax import lax
from jax.experimental import pallas as pl
from jax.experimental.pallas import tpu as pltpu

N_HEADS = 16
HEAD_DIM = 128
MOBA_BLOCK = 256
MOBA_TOPK = 3
PAGE_SIZE = 128
PAGES_PER_BLOCK = MOBA_BLOCK // PAGE_SIZE
CONV_WIDTH = 31
CONV_HALO = 32
LN_EPS = 1e-5
DEPTH = 1
DEEPNORM_ALPHA = (2.0 * DEPTH) ** 0.25
ATTN_SCALE = HEAD_DIM ** -0.5
LANES = 128
VMEM_LIMIT_BYTES = 56 * 1024 * 1024
NEG_INF = float("-inf")
BF16 = jnp.bfloat16
F32 = jnp.float32


def _params(*semantics):
    return pltpu.CompilerParams(dimension_semantics=semantics, vmem_limit_bytes=VMEM_LIMIT_BYTES)


def _sigmoid(z):
    return 1.0 / (1.0 + jnp.exp(-z))


def _layer_norm(x, g, b):
    mu = jnp.mean(x, axis=-1, keepdims=True)
    xc = x - mu
    var = jnp.mean(xc * xc, axis=-1, keepdims=True)
    return xc * lax.rsqrt(var + LN_EPS) * g + b


def _mm(a, b):
    return jnp.dot(a, b, preferred_element_type=F32)


def _proj_plain_kernel(x_ref, w_ref, b_ref, o_ref, *, sigmoid):
    z = _mm(x_ref[...], w_ref[...]) + b_ref[...]
    o_ref[...] = _sigmoid(z) if sigmoid else z


def _proj_glu_kernel(x_ref, wa_ref, wg_ref, ba_ref, bg_ref, o_ref):
    x = x_ref[...]
    a = _mm(x, wa_ref[...]) + ba_ref[...]
    g = _mm(x, wg_ref[...]) + bg_ref[...]
    o_ref[...] = a * _sigmoid(g)


def _proj_qT_kernel(x_ref, w_ref, b_ref, qT_ref):
    z = _mm(x_ref[...], w_ref[...]) + b_ref[...]
    qT_ref[...] = z.T


def _proj_k_kernel(x_ref, w_ref, b_ref, kf_ref, kb_ref, kbar_ref):
    z = _mm(x_ref[...], w_ref[...]) + b_ref[...]
    kf_ref[...] = z
    kb_ref[...] = z.astype(BF16)
    tm, tn = z.shape
    kbar_ref[0] = jnp.sum(z.reshape(tm // MOBA_BLOCK, MOBA_BLOCK, tn), axis=1) * (1.0 / MOBA_BLOCK)


def _proj_v_kernel(x_ref, w_ref, b_ref, vf_ref, vT_ref):
    z = _mm(x_ref[...], w_ref[...]) + b_ref[...]
    vf_ref[...] = z
    zt = z.T
    for c in range(vT_ref.shape[0]):
        vT_ref[c] = zt[:, c * MOBA_BLOCK:(c + 1) * MOBA_BLOCK].astype(BF16)


def _proj_specs(m, k, tm, tn, col_block):
    x_spec = pl.BlockSpec((tm, k), lambda i, j: (i, 0))
    w_spec = pl.BlockSpec((k, tn), lambda i, j: (0, col_block + j))
    b_spec = pl.BlockSpec((1, tn), lambda i, j: (0, col_block + j))
    return x_spec, w_spec, b_spec


def _proj_plain(x, w, b, col0, n, tm, tn, sigmoid=False):
    m, k = x.shape
    x_spec, w_spec, b_spec = _proj_specs(m, k, tm, tn, col0 // tn)
    return pl.pallas_call(
        functools.partial(_proj_plain_kernel, sigmoid=sigmoid),
        grid=(m // tm, n // tn),
        in_specs=[x_spec, w_spec, b_spec],
        out_specs=pl.BlockSpec((tm, tn), lambda i, j: (i, j)),
        out_shape=jax.ShapeDtypeStruct((m, n), F32),
        compiler_params=_params("parallel", "parallel"),
        name="proj_plain",
    )(x, w, b)


def _proj_glu(x, w, b, col_a, col_g, n, tm, tn):
    m, k = x.shape
    x_spec, wa_spec, ba_spec = _proj_specs(m, k, tm, tn, col_a // tn)
    _, wg_spec, bg_spec = _proj_specs(m, k, tm, tn, col_g // tn)
    return pl.pallas_call(
        _proj_glu_kernel,
        grid=(m // tm, n // tn),
        in_specs=[x_spec, wa_spec, wg_spec, ba_spec, bg_spec],
        out_specs=pl.BlockSpec((tm, tn), lambda i, j: (i, j)),
        out_shape=jax.ShapeDtypeStruct((m, n), F32),
        compiler_params=_params("parallel", "parallel"),
        name="proj_glu",
    )(x, w, w, b, b)


def _proj_qT(x, w, b, col0, n, tm, tn):
    m, k = x.shape
    x_spec, w_spec, b_spec = _proj_specs(m, k, tm, tn, col0 // tn)
    return pl.pallas_call(
        _proj_qT_kernel,
        grid=(m // tm, n // tn),
        in_specs=[x_spec, w_spec, b_spec],
        out_specs=pl.BlockSpec((tn, tm), lambda i, j: (j, i)),
        out_shape=jax.ShapeDtypeStruct((n, m), F32),
        compiler_params=_params("parallel", "parallel"),
        name="proj_qT",
    )(x, w, b)


def _proj_k(x, w, b, col0, n, tm, tn):
    m, k = x.shape
    x_spec, w_spec, b_spec = _proj_specs(m, k, tm, tn, col0 // tn)
    nb = tm // MOBA_BLOCK
    return pl.pallas_call(
        _proj_k_kernel,
        grid=(m // tm, n // tn),
        in_specs=[x_spec, w_spec, b_spec],
        out_specs=[pl.BlockSpec((tm, tn), lambda i, j: (i, j)),
                   pl.BlockSpec((tm, tn), lambda i, j: (i, j)),
                   pl.BlockSpec((1, nb, tn), lambda i, j: (i, 0, j))],
        out_shape=[jax.ShapeDtypeStruct((m, n), F32),
                   jax.ShapeDtypeStruct((m, n), BF16),
                   jax.ShapeDtypeStruct((m // tm, nb, n), F32)],
        compiler_params=_params("parallel", "parallel"),
        name="proj_k",
    )(x, w, b)


def _proj_v(x, w, b, col0, n, tm, tn):
    m, k = x.shape
    x_spec, w_spec, b_spec = _proj_specs(m, k, tm, tn, col0 // tn)
    nb = tm // MOBA_BLOCK
    return pl.pallas_call(
        _proj_v_kernel,
        grid=(m // tm, n // tn),
        in_specs=[x_spec, w_spec, b_spec],
        out_specs=[pl.BlockSpec((tm, tn), lambda i, j: (i, j)),
                   pl.BlockSpec((nb, tn, MOBA_BLOCK), lambda i, j: (i, j, 0))],
        out_shape=[jax.ShapeDtypeStruct((m, n), F32),
                   jax.ShapeDtypeStruct((m // MOBA_BLOCK, n, MOBA_BLOCK), BF16)],
        compiler_params=_params("parallel", "parallel"),
        name="proj_v",
    )(x, w, b)


def _top3_blocks(scores, blk_row):
    picks = []
    for _ in range(MOBA_TOPK):
        mx = jnp.max(scores, axis=0, keepdims=True)
        idx = jnp.min(jnp.where(scores == mx, blk_row, 1 << 20), axis=0, keepdims=True)
        picks.append(jnp.where(mx > NEG_INF, idx, -1))
        scores = jnp.where(blk_row == idx, NEG_INF, scores)
    return picks


def _attn_kernel(slopes_ref, qT_ref, k_ref, vT_ref, kbar_ref, o_ref):
    h = pl.program_id(1)
    i = pl.program_id(2)
    n_blk = k_ref.shape[0]
    qT = qT_ref[...]
    gate = lax.dot_general(kbar_ref[0], qT, (((1,), (0,)), ((), ())),
                           precision=lax.Precision.HIGHEST, preferred_element_type=F32)
    blk_row = lax.broadcasted_iota(jnp.int32, (n_blk, MOBA_BLOCK), 0)
    gate = jnp.where(blk_row < i, gate, NEG_INF)
    sel0, sel1, sel2 = _top3_blocks(gate, blk_row)

    qTb = qT.astype(BF16)
    neg_slope = -jnp.full((1, MOBA_BLOCK), slopes_ref[h], F32)
    key_pos = lax.broadcasted_iota(jnp.int32, (MOBA_BLOCK, MOBA_BLOCK), 0)
    qry_pos = lax.broadcasted_iota(jnp.int32, (MOBA_BLOCK, MOBA_BLOCK), 1)
    bias = (qry_pos - key_pos).astype(F32) * neg_slope

    s = _mm(k_ref[i], qTb) * ATTN_SCALE + bias
    s = jnp.where(key_pos <= qry_pos, s, NEG_INF)
    m = jnp.max(s, axis=0, keepdims=True)
    p = jnp.exp(s - m)
    l = jnp.sum(p, axis=0, keepdims=True)
    acc = _mm(vT_ref[i], p.astype(BF16))

    def body(j, carry):
        m, l, acc = carry
        s = _mm(k_ref[j], qTb) * ATTN_SCALE + bias
        picked = (sel0 == j) | (sel1 == j) | (sel2 == j)
        s = jnp.where(picked, s, NEG_INF)
        off = jnp.full((1, MOBA_BLOCK), (i - j) * MOBA_BLOCK, jnp.int32).astype(F32) * neg_slope
        m_new = jnp.maximum(m, jnp.max(s, axis=0, keepdims=True) + off)
        alpha = jnp.exp(m - m_new)
        p = jnp.exp(s - (m_new - off))
        l = alpha * l + jnp.sum(p, axis=0, keepdims=True)
        acc = alpha * acc + _mm(vT_ref[j], p.astype(BF16))
        return m_new, l, acc

    m, l, acc = lax.fori_loop(0, i, body, (m, l, acc))
    o_ref[...] = (acc / l).T.astype(o_ref.dtype)


def _moba_prompt(qT, kb, vT, kbar, slopes, batch, n_blk):
    d, m = qT.shape
    return pl.pallas_call(
        _attn_kernel,
        grid=(batch, N_HEADS, n_blk),
        in_specs=[pl.BlockSpec(memory_space=pltpu.SMEM),
                  pl.BlockSpec((HEAD_DIM, MOBA_BLOCK), lambda b, h, i: (h, b * n_blk + i)),
                  pl.BlockSpec((n_blk, MOBA_BLOCK, HEAD_DIM), lambda b, h, i: (b, 0, h)),
                  pl.BlockSpec((n_blk, HEAD_DIM, MOBA_BLOCK), lambda b, h, i: (b, h, 0)),
                  pl.BlockSpec((1, n_blk, HEAD_DIM), lambda b, h, i: (b, 0, h))],
        out_specs=pl.BlockSpec((MOBA_BLOCK, HEAD_DIM), lambda b, h, i: (b * n_blk + i, h)),
        out_shape=jax.ShapeDtypeStruct((m, d), BF16),
        compiler_params=_params("parallel", "parallel", "parallel"),
        name="moba_prompt",
    )(slopes, qT, kb, vT, kbar)


CONV_ROW_CHUNK = 32


def _dwconv_kernel(cur_ref, prev_ref, w_ref, b_ref, o_ref, ext_ref, *, tiles_per_seq):
    i = pl.program_id(0)
    tm, tc = cur_ref.shape
    starts_sequence = (i % tiles_per_seq) == 0
    ext_ref[0:CONV_HALO, :] = jnp.where(starts_sequence, 0.0, prev_ref[...])
    ext_ref[CONV_HALO:, :] = cur_ref[...]
    first = CONV_HALO - (CONV_WIDTH - 1)
    for r0 in range(0, tm, CONV_ROW_CHUNK):
        for c0 in range(0, tc, LANES):
            acc = jnp.broadcast_to(b_ref[:, c0:c0 + LANES], (CONV_ROW_CHUNK, LANES))
            for k in range(CONV_WIDTH):
                acc = acc + w_ref[k:k + 1, c0:c0 + LANES] * ext_ref[r0 + first + k:r0 + first + k + CONV_ROW_CHUNK, c0:c0 + LANES]
            o_ref[r0:r0 + CONV_ROW_CHUNK, c0:c0 + LANES] = acc


def _dwconv_prompt(u, w_dw, b_dw, seq, tm, tc):
    m, c = u.shape
    halo_per_tile = tm // CONV_HALO
    return pl.pallas_call(
        functools.partial(_dwconv_kernel, tiles_per_seq=seq // tm),
        grid=(m // tm, c // tc),
        in_specs=[pl.BlockSpec((tm, tc), lambda i, j: (i, j)),
                  pl.BlockSpec((CONV_HALO, tc), lambda i, j: (jnp.maximum(i * halo_per_tile - 1, 0), j)),
                  pl.BlockSpec((CONV_WIDTH, tc), lambda i, j: (0, j)),
                  pl.BlockSpec((1, tc), lambda i, j: (0, j))],
        out_specs=pl.BlockSpec((tm, tc), lambda i, j: (i, j)),
        out_shape=jax.ShapeDtypeStruct((m, c), F32),
        scratch_shapes=[pltpu.VMEM((tm + CONV_HALO, tc), F32)],
        compiler_params=_params("parallel", "parallel"),
        name="dwconv_prompt",
    )(u, u, w_dw, b_dw)


def _dwconv_sample_kernel(u_ref, w_ref, b_ref, o_ref):
    for t in range(o_ref.shape[0]):
        acc = jnp.broadcast_to(b_ref[...], o_ref.shape[1:])
        for k in range(CONV_WIDTH):
            acc = acc + w_ref[k:k + 1, :] * u_ref[t + k]
        o_ref[t] = acc


def _dwconv_sample(u_tmajor, w_dw, b_dw, n_new, tc):
    t_ext, nb, c = u_tmajor.shape
    return pl.pallas_call(
        _dwconv_sample_kernel,
        grid=(c // tc,),
        in_specs=[pl.BlockSpec((t_ext, nb, tc), lambda j: (0, 0, j)),
                  pl.BlockSpec((CONV_WIDTH, tc), lambda j: (0, j)),
                  pl.BlockSpec((1, tc), lambda j: (0, j))],
        out_specs=pl.BlockSpec((n_new, nb, tc), lambda j: (0, 0, j)),
        out_shape=jax.ShapeDtypeStruct((n_new, nb, c), F32),
        compiler_params=_params("parallel"),
        name="dwconv_sample",
    )(u_tmajor, w_dw, b_dw)


def _ln_silu_mm_kernel(x_ref, g_ref, b_ref, w_ref, o_ref):
    y = _layer_norm(x_ref[...], g_ref[...], b_ref[...])
    hidden = y * _sigmoid(y)
    o_ref[...] = _mm(hidden.astype(BF16), w_ref[...])


def _resident(shape):
    return pl.BlockSpec(shape, lambda *_: (0,) * len(shape), pipeline_mode=pl.Buffered(1))


def _ln_silu_mm(x, g, b, w, tm):
    m, c = x.shape
    n = w.shape[1]
    return pl.pallas_call(
        _ln_silu_mm_kernel,
        grid=(m // tm,),
        in_specs=[pl.BlockSpec((tm, c), lambda i: (i, 0)),
                  _resident((1, c)), _resident((1, c)), _resident((c, n))],
        out_specs=pl.BlockSpec((tm, n), lambda i: (i, 0)),
        out_shape=jax.ShapeDtypeStruct((m, n), F32),
        compiler_params=_params("parallel"),
        name="ln_silu_mm",
    )(x, g, b, w)


def _merge_kernel(oa_ref, ga_ref, gb_ref, oc_ref, x_ref, wa_ref, wo_ref, g_ref, b_ref, h_ref):
    br_attn = _mm(oa_ref[...], wa_ref[...])
    mixed = ga_ref[...] * br_attn + gb_ref[...] * oc_ref[...]
    y = DEEPNORM_ALPHA * x_ref[...] + _mm(mixed.astype(BF16), wo_ref[...])
    h_ref[...] = _layer_norm(y, g_ref[...], b_ref[...])


def _merge(o_attn, gates, o_conv, x, w_attn_o, w_out, g, b, tm):
    m, d = x.shape
    row = lambda i: (i, 0)
    return pl.pallas_call(
        _merge_kernel,
        grid=(m // tm,),
        in_specs=[pl.BlockSpec((tm, d), row),
                  pl.BlockSpec((tm, d), lambda i: (i, 0)),
                  pl.BlockSpec((tm, d), lambda i: (i, 1)),
                  pl.BlockSpec((tm, d), row),
                  pl.BlockSpec((tm, d), row),
                  _resident((d, d)), _resident((d, d)), _resident((1, d)), _resident((1, d))],
        out_specs=pl.BlockSpec((tm, d), row),
        out_shape=jax.ShapeDtypeStruct((m, d), F32),
        compiler_params=_params("parallel"),
        name="merge",
    )(o_attn, gates, gates, o_conv, x, w_attn_o, w_out, g, b)


def _mlp_kernel(h_ref, w1_ref, w2_ref, g_ref, b_ref, o_ref, hb_ref, acc_ref):
    f = pl.program_id(1)

    @pl.when(f == 0)
    def _():
        hb_ref[...] = h_ref[...].astype(BF16)
        acc_ref[...] = jnp.zeros_like(acc_ref)

    a = _mm(hb_ref[...], w1_ref[...])
    a = jnp.square(jnp.maximum(a, 0.0))
    acc_ref[...] += _mm(a.astype(BF16), w2_ref[...])

    @pl.when(f == pl.num_programs(1) - 1)
    def _():
        y = DEEPNORM_ALPHA * h_ref[...] + acc_ref[...]
        o_ref[...] = _layer_norm(y, g_ref[...], b_ref[...])


def _mlp(h, w1, w2, g, b, tm, tf):
    m, d = h.shape
    ff = w1.shape[1]
    return pl.pallas_call(
        _mlp_kernel,
        grid=(m // tm, ff // tf),
        in_specs=[pl.BlockSpec((tm, d), lambda i, f: (i, 0)),
                  pl.BlockSpec((d, tf), lambda i, f: (0, f)),
                  pl.BlockSpec((tf, d), lambda i, f: (f, 0)),
                  pl.BlockSpec((1, d), lambda i, f: (0, 0)),
                  pl.BlockSpec((1, d), lambda i, f: (0, 0))],
        out_specs=pl.BlockSpec((tm, d), lambda i, f: (i, 0)),
        out_shape=jax.ShapeDtypeStruct((m, d), F32),
        scratch_shapes=[pltpu.VMEM((tm, d), BF16), pltpu.VMEM((tm, d), F32)],
        compiler_params=_params("parallel", "arbitrary"),
        name="mlp",
    )(h, w1, w2, g, b)


PAGES_PER_STEP = 8


def _block_mean_kernel(pt_ref, *refs):
    o_ref = refs[-1]
    for blk in range(PAGES_PER_STEP // PAGES_PER_BLOCK):
        total = None
        for p in range(PAGES_PER_BLOCK):
            part = jnp.sum(refs[blk * PAGES_PER_BLOCK + p][0, 0], axis=0)
            total = part if total is None else total + part
        o_ref[0, blk] = total * (1.0 / MOBA_BLOCK)


def _block_means(cache, page_table_flat, n_batch, n_pages):
    steps = n_pages // PAGES_PER_STEP
    blocks_per_step = PAGES_PER_STEP // PAGES_PER_BLOCK
    n_blk = steps * blocks_per_step

    def page_spec(p):
        return pl.BlockSpec((1, 1, PAGE_SIZE, N_HEADS, HEAD_DIM),
                            lambda b, g, pt: (0, pt[b * n_pages + g * PAGES_PER_STEP + p], 0, 0, 0))

    out = pl.pallas_call(
        _block_mean_kernel,
        grid_spec=pltpu.PrefetchScalarGridSpec(
            num_scalar_prefetch=1,
            grid=(n_batch, steps),
            in_specs=[page_spec(p) for p in range(PAGES_PER_STEP)],
            out_specs=pl.BlockSpec((1, blocks_per_step, N_HEADS, HEAD_DIM),
                                   lambda b, g, pt: (b * steps + g, 0, 0, 0))),
        out_shape=jax.ShapeDtypeStruct((n_batch * steps, blocks_per_step, N_HEADS, HEAD_DIM), F32),
        compiler_params=_params("parallel", "parallel"),
        name="block_means",
    )(page_table_flat, *([cache] * PAGES_PER_STEP))
    return out.reshape(n_batch, n_blk, N_HEADS * HEAD_DIM)


def _sample_gate_kernel(q_ref, kbar_ref, o_ref):
    n_new = q_ref.shape[1]
    n_blk = kbar_ref.shape[1]
    kbar = kbar_ref[0]
    blk_row = lax.broadcasted_iota(jnp.int32, (n_blk, N_HEADS), 0)
    head_col = lax.broadcasted_iota(jnp.int32, (n_blk, N_HEADS), 1)
    for t in range(n_new):
        prod = kbar * q_ref[0, t:t + 1, :]
        scores = jnp.zeros((n_blk, N_HEADS), F32)
        for h in range(N_HEADS):
            head_score = jnp.sum(prod[:, h * HEAD_DIM:(h + 1) * HEAD_DIM], axis=1, keepdims=True)
            scores = jnp.where(head_col == h, head_score, scores)
        picks = _top3_blocks(scores, blk_row)
        for r in range(MOBA_TOPK):
            o_ref[0, t * MOBA_TOPK + r:t * MOBA_TOPK + r + 1, :] = picks[r]


def _sample_gate(q, kbar):
    nb, n_new, d = q.shape
    n_blk = kbar.shape[1]
    return pl.pallas_call(
        _sample_gate_kernel,
        grid=(nb,),
        in_specs=[pl.BlockSpec((1, n_new, d), lambda b: (b, 0, 0)),
                  pl.BlockSpec((1, n_blk, d), lambda b: (b, 0, 0))],
        out_specs=pl.BlockSpec((1, n_new * MOBA_TOPK, N_HEADS), lambda b: (b, 0, 0)),
        out_shape=jax.ShapeDtypeStruct((nb, n_new * MOBA_TOPK, N_HEADS), jnp.int32),
        compiler_params=_params("parallel"),
        name="sample_gate",
    )(q, kbar)


N_SEL_PAGES = MOBA_TOPK * PAGES_PER_BLOCK


def _sample_attn_kernel(sel_ref, pt_ref, slopes_ref, q_ref, ko_ref, vo_ref, *refs, past_len):
    k_pages = refs[:N_SEL_PAGES]
    v_pages = refs[N_SEL_PAGES:2 * N_SEL_PAGES]
    o_ref = refs[-1]
    b, h, t = pl.program_id(0), pl.program_id(1), pl.program_id(2)
    n_new = q_ref.shape[1]
    slope = jnp.full((1, PAGE_SIZE), slopes_ref[h], F32)
    q = q_ref[0, pl.ds(t, 1), :]
    q8 = jnp.broadcast_to(q, (8, HEAD_DIM)).astype(BF16)
    lane = lax.broadcasted_iota(jnp.int32, (1, PAGE_SIZE), 1)
    sel_base = ((b * N_HEADS + h) * n_new + t) * MOBA_TOPK

    logits = []
    for r in range(MOBA_TOPK):
        blk = sel_ref[sel_base + r]
        for p in range(PAGES_PER_BLOCK):
            kp = k_pages[r * PAGES_PER_BLOCK + p][0, :, 0, 0, :].astype(BF16)
            s = lax.dot_general(q8, kp, (((1,), (1,)), ((), ())), preferred_element_type=F32)[0:1]
            dist = (past_len + t - blk * MOBA_BLOCK - p * PAGE_SIZE) - lane
            logits.append(s * ATTN_SCALE - slope * dist.astype(F32))
    row = lax.broadcasted_iota(jnp.int32, (n_new, 1), 0)
    s_own = jnp.sum(ko_ref[0] * q, axis=1, keepdims=True)
    l_own = s_own * ATTN_SCALE - slope[:, 0:1] * (t - row).astype(F32)
    l_own = jnp.where(row <= t, l_own, NEG_INF)

    m = jnp.max(l_own, axis=0, keepdims=True)
    for lg in logits:
        m = jnp.maximum(m, jnp.max(lg, axis=1, keepdims=True))
    p_own = jnp.exp(l_own - m)
    denom = jnp.sum(p_own, axis=0, keepdims=True)
    out = jnp.sum(p_own * vo_ref[0], axis=0, keepdims=True)
    for idx, lg in enumerate(logits):
        p = jnp.exp(lg - m)
        denom = denom + jnp.sum(p, axis=1, keepdims=True)
        p8 = jnp.broadcast_to(p, (8, PAGE_SIZE)).astype(BF16)
        out = out + _mm(p8, v_pages[idx][0, :, 0, 0, :].astype(BF16))[0:1]
    o_ref[0, pl.ds(t, 1), :] = out / denom


def _sample_attn(sel_flat, page_table_flat, slopes, q, k_new, v_new, cache_k, cache_v, n_pages, past_len):
    nb, n_new, d = q.shape

    def page_spec(r, p):
        def index(b, h, t, sel, pt):
            blk = sel[((b * N_HEADS + h) * n_new + t) * MOBA_TOPK + r]
            return (pt[b * n_pages + blk * PAGES_PER_BLOCK + p], 0, h, 0, 0)
        return pl.BlockSpec((1, PAGE_SIZE, 1, 1, HEAD_DIM), index)

    page_specs = [page_spec(r, p) for r in range(MOBA_TOPK) for p in range(PAGES_PER_BLOCK)]
    head_spec = pl.BlockSpec((1, n_new, HEAD_DIM), lambda b, h, t, sel, pt: (b, 0, h))
    return pl.pallas_call(
        functools.partial(_sample_attn_kernel, past_len=past_len),
        grid_spec=pltpu.PrefetchScalarGridSpec(
            num_scalar_prefetch=2,
            grid=(nb, N_HEADS, n_new),
            in_specs=[pl.BlockSpec(memory_space=pltpu.SMEM), head_spec, head_spec, head_spec]
                     + page_specs + page_specs,
            out_specs=head_spec),
        out_shape=jax.ShapeDtypeStruct((nb, n_new, d), F32),
        compiler_params=_params("parallel", "parallel", "arbitrary"),
        name="sample_attn",
    )(sel_flat, page_table_flat, slopes, q, k_new, v_new,
      *([cache_k] * N_SEL_PAGES), *([cache_v] * N_SEL_PAGES))


def _alibi_slopes():
    return jnp.asarray(2.0 ** (-8.0 * np.arange(1, N_HEADS + 1) / N_HEADS), F32)


def kernel(x_prompt, x_sample, cache_k, cache_v, state_conv, page_table, w_in, b_in, w_attn_o,
           w_dw, b_dw, ln_conv_g, ln_conv_b, w_conv_o, w_out, ln1_g, ln1_b, w_mlp1, w_mlp2,
           ln2_g, ln2_b):
    batch, seq, d = x_prompt.shape
    dec_batch, dec_seq, _ = x_sample.shape
    n_pages = page_table.shape[1]
    past_len = n_pages * PAGE_SIZE
    n_blk = seq // MOBA_BLOCK
    assert w_in.shape[0] == DEPTH and d == N_HEADS * HEAD_DIM
    assert seq % MOBA_BLOCK == 0 and past_len % MOBA_BLOCK == 0 and n_pages % PAGES_PER_STEP == 0

    row2 = lambda a: a[0].reshape(1, -1)
    w_in_b = w_in[0].astype(BF16)
    b_in2 = row2(b_in)
    w_attn_o_b, w_conv_o_b, w_out_b = w_attn_o[0].astype(BF16), w_conv_o[0].astype(BF16), w_out[0].astype(BF16)
    w_mlp1_b, w_mlp2_b = w_mlp1[0].astype(BF16), w_mlp2[0].astype(BF16)
    w_dw2, b_dw2 = w_dw[0], row2(b_dw)
    slopes = _alibi_slopes()
    c_q, c_k, c_v, c_a, c_g, c_gate = 0, d, 2 * d, 3 * d, 4 * d, 5 * d

    def tail(x, o_attn, gates, o_conv, tm_merge, tm_mlp):
        h = _merge(o_attn, gates, o_conv, x, w_attn_o_b, w_out_b, row2(ln1_g), row2(ln1_b), tm_merge)
        return _mlp(h, w_mlp1_b, w_mlp2_b, row2(ln2_g), row2(ln2_b), tm_mlp, 512)

    m_p = batch * seq
    xp = x_prompt.reshape(m_p, d)
    xp_b = xp.astype(BF16)
    tm, tn = 1024, 512
    qT = _proj_qT(xp_b, w_in_b, b_in2, c_q, d, tm, tn)
    k_f, k_b, kbar = _proj_k(xp_b, w_in_b, b_in2, c_k, d, tm, tn)
    v_f, vT = _proj_v(xp_b, w_in_b, b_in2, c_v, d, tm, tn)
    glu = _proj_glu(xp_b, w_in_b, b_in2, c_a, c_g, d, tm, tn)
    gates = _proj_plain(xp_b, w_in_b, b_in2, c_gate, 2 * d, tm, tn, sigmoid=True)

    o_attn = _moba_prompt(qT, k_b.reshape(m_p // MOBA_BLOCK, MOBA_BLOCK, d), vT,
                          kbar.reshape(batch, n_blk, d), slopes, batch, n_blk)
    conv = _dwconv_prompt(glu, w_dw2, b_dw2, seq, 256, 512)
    o_conv = _ln_silu_mm(conv, row2(ln_conv_g), row2(ln_conv_b), w_conv_o_b, 256)
    y_prompt = tail(xp, o_attn, gates, o_conv, 256, 512)

    new_k_prompt = k_f.reshape(DEPTH, batch, seq // PAGE_SIZE, PAGE_SIZE, N_HEADS, HEAD_DIM)
    new_v_prompt = v_f.reshape(DEPTH, batch, seq // PAGE_SIZE, PAGE_SIZE, N_HEADS, HEAD_DIM)
    new_conv_prompt = glu.reshape(batch, seq, d)[:, seq - (CONV_WIDTH - 1):][None]

    m_s = dec_batch * dec_seq
    xs = x_sample.reshape(m_s, d)
    xs_b = xs.astype(BF16)
    qkv_s = _proj_plain(xs_b, w_in_b, b_in2, c_q, 3 * d, m_s, tn)
    q_s = qkv_s[:, :d].reshape(dec_batch, dec_seq, d)
    k_s = qkv_s[:, d:2 * d].reshape(dec_batch, dec_seq, d)
    v_s = qkv_s[:, 2 * d:].reshape(dec_batch, dec_seq, d)
    glu_s = _proj_glu(xs_b, w_in_b, b_in2, c_a, c_g, d, m_s, tn)
    gates_s = _proj_plain(xs_b, w_in_b, b_in2, c_gate, 2 * d, m_s, tn, sigmoid=True)

    ck = cache_k[0].reshape(cache_k.shape[1], PAGE_SIZE, N_HEADS, 1, HEAD_DIM)
    cv = cache_v[0].reshape(cache_v.shape[1], PAGE_SIZE, N_HEADS, 1, HEAD_DIM)
    pt_flat = page_table.reshape(-1)
    kbar_s = _block_means(cache_k, pt_flat, dec_batch, n_pages)
    sel = _sample_gate(q_s, kbar_s)
    sel_flat = sel.reshape(dec_batch, dec_seq, MOBA_TOPK, N_HEADS).transpose(0, 3, 1, 2).reshape(-1)
    o_attn_s = _sample_attn(sel_flat, pt_flat, slopes, q_s, k_s, v_s, ck, cv, n_pages, past_len)

    u_ext = jnp.concatenate([state_conv[0], glu_s.reshape(dec_batch, dec_seq, d)], axis=1)
    conv_s = _dwconv_sample(u_ext.transpose(1, 0, 2), w_dw2, b_dw2, dec_seq, 512)
    conv_s = conv_s.transpose(1, 0, 2).reshape(m_s, d)
    o_conv_s = _ln_silu_mm(conv_s, row2(ln_conv_g), row2(ln_conv_b), w_conv_o_b, m_s)
    y_sample = tail(xs, o_attn_s.reshape(m_s, d).astype(BF16), gates_s, o_conv_s, m_s, m_s)

    new_k_sample = k_s.reshape(DEPTH, dec_batch, dec_seq, N_HEADS, HEAD_DIM)
    new_v_sample = v_s.reshape(DEPTH, dec_batch, dec_seq, N_HEADS, HEAD_DIM)
    new_conv_sample = u_ext[:, dec_seq:][None]

    return (y_prompt.reshape(batch, seq, d), y_sample.reshape(dec_batch, dec_seq, d),
            new_k_prompt, new_v_prompt, new_conv_prompt, new_k_sample, new_v_sample, new_conv_sample)
```

```python
import functools

import numpy as np
import jax
import jax.numpy as jnp
from jax import lax
from jax.experimental import pallas as pl
from jax.experimental.pallas import tpu as pltpu

N_HEADS = 16
HEAD_DIM = 128
MOBA_BLOCK = 256
MOBA_TOPK = 3
PAGE_SIZE = 128
PAGES_PER_BLOCK = MOBA_BLOCK // PAGE_SIZE
CONV_WIDTH = 31
CONV_HALO = 32
LN_EPS = 1e-5
DEPTH = 1
DEEPNORM_ALPHA = (2.0 * DEPTH) ** 0.25
ATTN_SCALE = HEAD_DIM ** -0.5
LOG2E = 1.4426950408889634
ATTN_TILES = 4
LANES = 128
SUBLANES = 8
VMEM_LIMIT_BYTES = 56 * 1024 * 1024
NEG_INF = float("-inf")
BF16 = jnp.bfloat16
F32 = jnp.float32


def _params(*semantics):
    return pltpu.CompilerParams(dimension_semantics=semantics, vmem_limit_bytes=VMEM_LIMIT_BYTES)


def _sigmoid(z):
    return 1.0 / (1.0 + jnp.exp(-z))


def _layer_norm(x, g, b):
    mu = jnp.mean(x, axis=-1, keepdims=True)
    xc = x - mu
    var = jnp.mean(xc * xc, axis=-1, keepdims=True)
    return xc * lax.rsqrt(var + LN_EPS) * g + b


def _mm(a, b):
    return jnp.dot(a, b, preferred_element_type=F32)


def _proj_plain_kernel(x_ref, w_ref, b_ref, o_ref, *, sigmoid):
    z = _mm(x_ref[...], w_ref[...]) + b_ref[...]
    o_ref[...] = _sigmoid(z) if sigmoid else z


def _proj_glu_kernel(x_ref, wa_ref, wg_ref, ba_ref, bg_ref, o_ref):
    x = x_ref[...]
    a = _mm(x, wa_ref[...]) + ba_ref[...]
    g = _mm(x, wg_ref[...]) + bg_ref[...]
    o_ref[...] = a * _sigmoid(g)


def _proj_qT_kernel(x_ref, w_ref, b_ref, qT_ref):
    z = _mm(x_ref[...], w_ref[...]) + b_ref[...]
    qT_ref[...] = z.T


def _proj_k_kernel(x_ref, w_ref, b_ref, kf_ref, kb_ref, kbar_ref):
    z = _mm(x_ref[...], w_ref[...]) + b_ref[...]
    kf_ref[...] = z
    kb_ref[...] = z.astype(BF16)
    tm, tn = z.shape
    kbar_ref[0] = jnp.sum(z.reshape(tm // MOBA_BLOCK, MOBA_BLOCK, tn), axis=1) * (1.0 / MOBA_BLOCK)


def _proj_v_kernel(x_ref, w_ref, b_ref, vf_ref, vT_ref):
    z = _mm(x_ref[...], w_ref[...]) + b_ref[...]
    vf_ref[...] = z
    zt = z.T
    for c in range(vT_ref.shape[0]):
        vT_ref[c] = zt[:, c * MOBA_BLOCK:(c + 1) * MOBA_BLOCK].astype(BF16)


def _proj_specs(m, k, tm, tn, col_block):
    x_spec = pl.BlockSpec((tm, k), lambda i, j: (i, 0))
    w_spec = pl.BlockSpec((k, tn), lambda i, j: (0, col_block + j))
    b_spec = pl.BlockSpec((1, tn), lambda i, j: (0, col_block + j))
    return x_spec, w_spec, b_spec


def _proj_plain(x, w, b, col0, n, tm, tn, sigmoid=False):
    m, k = x.shape
    x_spec, w_spec, b_spec = _proj_specs(m, k, tm, tn, col0 // tn)
    return pl.pallas_call(
        functools.partial(_proj_plain_kernel, sigmoid=sigmoid),
        grid=(m // tm, n // tn),
        in_specs=[x_spec, w_spec, b_spec],
        out_specs=pl.BlockSpec((tm, tn), lambda i, j: (i, j)),
        out_shape=jax.ShapeDtypeStruct((m, n), F32),
        compiler_params=_params("parallel", "parallel"),
        name="proj_plain",
    )(x, w, b)


def _proj_glu(x, w, b, col_a, col_g, n, tm, tn):
    m, k = x.shape
    x_spec, wa_spec, ba_spec = _proj_specs(m, k, tm, tn, col_a // tn)
    _, wg_spec, bg_spec = _proj_specs(m, k, tm, tn, col_g // tn)
    return pl.pallas_call(
        _proj_glu_kernel,
        grid=(m // tm, n // tn),
        in_specs=[x_spec, wa_spec, wg_spec, ba_spec, bg_spec],
        out_specs=pl.BlockSpec((tm, tn), lambda i, j: (i, j)),
        out_shape=jax.ShapeDtypeStruct((m, n), F32),
        compiler_params=_params("parallel", "parallel"),
        name="proj_glu",
    )(x, w, w, b, b)


def _proj_qT(x, w, b, col0, n, tm, tn):
    m, k = x.shape
    x_spec, w_spec, b_spec = _proj_specs(m, k, tm, tn, col0 // tn)
    return pl.pallas_call(
        _proj_qT_kernel,
        grid=(m // tm, n // tn),
        in_specs=[x_spec, w_spec, b_spec],
        out_specs=pl.BlockSpec((tn, tm), lambda i, j: (j, i)),
        out_shape=jax.ShapeDtypeStruct((n, m), F32),
        compiler_params=_params("parallel", "parallel"),
        name="proj_qT",
    )(x, w, b)


def _proj_k(x, w, b, col0, n, tm, tn):
    m, k = x.shape
    x_spec, w_spec, b_spec = _proj_specs(m, k, tm, tn, col0 // tn)
    nb = tm // MOBA_BLOCK
    return pl.pallas_call(
        _proj_k_kernel,
        grid=(m // tm, n // tn),
        in_specs=[x_spec, w_spec, b_spec],
        out_specs=[pl.BlockSpec((tm, tn), lambda i, j: (i, j)),
                   pl.BlockSpec((tm, tn), lambda i, j: (i, j)),
                   pl.BlockSpec((1, nb, tn), lambda i, j: (i, 0, j))],
        out_shape=[jax.ShapeDtypeStruct((m, n), F32),
                   jax.ShapeDtypeStruct((m, n), BF16),
                   jax.ShapeDtypeStruct((m // tm, nb, n), F32)],
        compiler_params=_params("parallel", "parallel"),
        name="proj_k",
    )(x, w, b)


def _proj_v(x, w, b, col0, n, tm, tn):
    m, k = x.shape
    x_spec, w_spec, b_spec = _proj_specs(m, k, tm, tn, col0 // tn)
    nb = tm // MOBA_BLOCK
    return pl.pallas_call(
        _proj_v_kernel,
        grid=(m // tm, n // tn),
        in_specs=[x_spec, w_spec, b_spec],
        out_specs=[pl.BlockSpec((tm, tn), lambda i, j: (i, j)),
                   pl.BlockSpec((nb, tn, MOBA_BLOCK), lambda i, j: (i, j, 0))],
        out_shape=[jax.ShapeDtypeStruct((m, n), F32),
                   jax.ShapeDtypeStruct((m // MOBA_BLOCK, n, MOBA_BLOCK), BF16)],
        compiler_params=_params("parallel", "parallel"),
        name="proj_v",
    )(x, w, b)


def _top3_blocks(scores, blk_row):
    picks = []
    for _ in range(MOBA_TOPK):
        mx = jnp.max(scores, axis=0, keepdims=True)
        idx = jnp.min(jnp.where(scores == mx, blk_row, 1 << 20), axis=0, keepdims=True)
        picks.append(jnp.where(mx > NEG_INF, idx, -1))
        scores = jnp.where(blk_row == idx, NEG_INF, scores)
    return picks


def _attn_kernel(slopes_ref, qT_ref, k_ref, vT_ref, kbar_ref, o_ref, sa_ref, ta_ref, sb_ref, tb_ref):
    h = pl.program_id(1)
    i = pl.program_id(2)
    n_blk = k_ref.shape[0]
    qT = qT_ref[...]
    gate = lax.dot_general(kbar_ref[0], qT, (((1,), (0,)), ((), ())),
                           precision=lax.Precision.HIGHEST, preferred_element_type=F32)
    blk_row = lax.broadcasted_iota(jnp.int32, (n_blk, MOBA_BLOCK), 0)
    gate = jnp.where(blk_row < i, gate, NEG_INF)
    sel0, sel1, sel2 = _top3_blocks(gate, blk_row)

    qTb = qT.astype(BF16)
    neg_slope2 = jnp.full((1, MOBA_BLOCK), slopes_ref[h], F32) * (-LOG2E)
    key_pos = lax.broadcasted_iota(jnp.int32, (MOBA_BLOCK, MOBA_BLOCK), 0)
    qry_pos = lax.broadcasted_iota(jnp.int32, (MOBA_BLOCK, MOBA_BLOCK), 1)
    bias = (qry_pos - key_pos).astype(F32) * neg_slope2
    scale2 = ATTN_SCALE * LOG2E

    def block_offset(j):
        return jnp.full((1, MOBA_BLOCK), (i - j) * MOBA_BLOCK, jnp.int32).astype(F32) * neg_slope2

    def score_stage(step, s_ref, tmax_ref):
        for u in range(ATTN_TILES):
            j = jnp.minimum(step * ATTN_TILES + u, n_blk - 1)
            s = _mm(k_ref[j], qTb) * scale2 + bias
            s_ref[u] = s
            tmax_ref[u] = jnp.max(s, axis=0, keepdims=True) + block_offset(j)

    def softmax_stage(step, s_ref, tmax_ref, m, l, acc):
        m_new = m
        picks = []
        for u in range(ATTN_TILES):
            j = step * ATTN_TILES + u
            picked = (sel0 == j) | (sel1 == j) | (sel2 == j)
            m_new = jnp.maximum(m_new, jnp.where(picked, tmax_ref[u], NEG_INF))
            picks.append((j, picked))
        alpha = jnp.exp2(m - m_new)
        l = alpha * l
        pv = None
        for u, (j, picked) in enumerate(picks):
            shift = jnp.where(picked, m_new - block_offset(j), jnp.inf)
            p = jnp.exp2(s_ref[u] - shift)
            l = l + jnp.sum(p, axis=0, keepdims=True)
            part = _mm(vT_ref[jnp.minimum(j, n_blk - 1)], p.astype(BF16))
            pv = part if pv is None else pv + part
        return m_new, l, alpha * acc + pv

    s = _mm(k_ref[i], qTb) * scale2 + bias
    s = jnp.where(key_pos <= qry_pos, s, NEG_INF)
    m = jnp.max(s, axis=0, keepdims=True)
    p = jnp.exp2(s - m)
    l = jnp.sum(p, axis=0, keepdims=True)
    acc = _mm(vT_ref[i], p.astype(BF16))

    score_stage(0, sa_ref, ta_ref)

    def body(pair, carry):
        step = 2 * pair
        carry = softmax_stage(step, sa_ref, ta_ref, *carry)
        score_stage(step + 1, sb_ref, tb_ref)
        carry = softmax_stage(step + 1, sb_ref, tb_ref, *carry)
        score_stage(step + 2, sa_ref, ta_ref)
        return carry

    n_pairs = (i + (2 * ATTN_TILES - 1)) // (2 * ATTN_TILES)
    m, l, acc = lax.fori_loop(0, n_pairs, body, (m, l, acc))
    o_ref[...] = (acc / l).T.astype(o_ref.dtype)


def _moba_prompt(qT, kb, vT, kbar, slopes, batch, n_blk):
    d, m = qT.shape
    return pl.pallas_call(
        _attn_kernel,
        grid=(batch, N_HEADS, n_blk),
        in_specs=[pl.BlockSpec(memory_space=pltpu.SMEM),
                  pl.BlockSpec((HEAD_DIM, MOBA_BLOCK), lambda b, h, i: (h, b * n_blk + i)),
                  pl.BlockSpec((n_blk, MOBA_BLOCK, HEAD_DIM), lambda b, h, i: (b, 0, h)),
                  pl.BlockSpec((n_blk, HEAD_DIM, MOBA_BLOCK), lambda b, h, i: (b, h, 0)),
                  pl.BlockSpec((1, n_blk, HEAD_DIM), lambda b, h, i: (b, 0, h))],
        out_specs=pl.BlockSpec((MOBA_BLOCK, HEAD_DIM), lambda b, h, i: (b * n_blk + i, h)),
        out_shape=jax.ShapeDtypeStruct((m, d), BF16),
        scratch_shapes=[pltpu.VMEM((ATTN_TILES, MOBA_BLOCK, MOBA_BLOCK), F32),
                        pltpu.VMEM((ATTN_TILES, 1, MOBA_BLOCK), F32)] * 2,
        compiler_params=_params("parallel", "parallel", "parallel"),
        name="moba_prompt",
    )(slopes, qT, kb, vT, kbar)


CONV_ROW_CHUNK = 32


def _dwconv_kernel(cur_ref, prev_ref, w_ref, b_ref, o_ref, ext_ref, shift_ref, *, tiles_per_seq):
    i = pl.program_id(0)
    tm, tc = cur_ref.shape
    starts_sequence = (i % tiles_per_seq) == 0
    ext_ref[0:CONV_HALO, :] = jnp.where(starts_sequence, 0.0, prev_ref[...])
    ext_ref[CONV_HALO:, :] = cur_ref[...]
    n_shifted = tm + CONV_HALO - SUBLANES
    for phase in range(1, SUBLANES):
        shift_ref[phase - 1, 0:n_shifted, :] = ext_ref[phase:phase + n_shifted, :]
    first = CONV_HALO - (CONV_WIDTH - 1)
    for c0 in range(0, tc, LANES):
        for r0 in range(0, tm, CONV_ROW_CHUNK):
            acc = jnp.broadcast_to(b_ref[:, c0:c0 + LANES], (CONV_ROW_CHUNK, LANES))
            for k in range(CONV_WIDTH):
                aligned, phase = divmod(first + k, SUBLANES)
                src = ext_ref if phase == 0 else shift_ref.at[phase - 1]
                row = r0 + aligned * SUBLANES
                acc = acc + w_ref[k:k + 1, c0:c0 + LANES] * src[row:row + CONV_ROW_CHUNK, c0:c0 + LANES]
            o_ref[r0:r0 + CONV_ROW_CHUNK, c0:c0 + LANES] = acc


def _dwconv_prompt(u, w_dw, b_dw, seq, tm, tc):
    m, c = u.shape
    halo_per_tile = tm // CONV_HALO
    return pl.pallas_call(
        functools.partial(_dwconv_kernel, tiles_per_seq=seq // tm),
        grid=(m // tm, c // tc),
        in_specs=[pl.BlockSpec((tm, tc), lambda i, j: (i, j)),
                  pl.BlockSpec((CONV_HALO, tc), lambda i, j: (jnp.maximum(i * halo_per_tile - 1, 0), j)),
                  pl.BlockSpec((CONV_WIDTH, tc), lambda i, j: (0, j)),
                  pl.BlockSpec((1, tc), lambda i, j: (0, j))],
        out_specs=pl.BlockSpec((tm, tc), lambda i, j: (i, j)),
        out_shape=jax.ShapeDtypeStruct((m, c), F32),
        scratch_shapes=[pltpu.VMEM((tm + CONV_HALO, tc), F32),
                        pltpu.VMEM((SUBLANES - 1, tm + CONV_HALO - SUBLANES, tc), F32)],
        compiler_params=_params("parallel", "parallel"),
        name="dwconv_prompt",
    )(u, u, w_dw, b_dw)


def _dwconv_sample_kernel(u_ref, w_ref, b_ref, o_ref):
    for t in range(o_ref.shape[0]):
        acc = jnp.broadcast_to(b_ref[...], o_ref.shape[1:])
        for k in range(CONV_WIDTH):
            acc = acc + w_ref[k:k + 1, :] * u_ref[t + k]
        o_ref[t] = acc


def _dwconv_sample(u_tmajor, w_dw, b_dw, n_new, tc):
    t_ext, nb, c = u_tmajor.shape
    return pl.pallas_call(
        _dwconv_sample_kernel,
        grid=(c // tc,),
        in_specs=[pl.BlockSpec((t_ext, nb, tc), lambda j: (0, 0, j)),
                  pl.BlockSpec((CONV_WIDTH, tc), lambda j: (0, j)),
                  pl.BlockSpec((1, tc), lambda j: (0, j))],
        out_specs=pl.BlockSpec((n_new, nb, tc), lambda j: (0, 0, j)),
        out_shape=jax.ShapeDtypeStruct((n_new, nb, c), F32),
        compiler_params=_params("parallel"),
        name="dwconv_sample",
    )(u_tmajor, w_dw, b_dw)


def _ln_silu_mm_kernel(x_ref, g_ref, b_ref, w_ref, o_ref):
    y = _layer_norm(x_ref[...], g_ref[...], b_ref[...])
    hidden = y * _sigmoid(y)
    o_ref[...] = _mm(hidden.astype(BF16), w_ref[...])


def _resident(shape):
    return pl.BlockSpec(shape, lambda *_: (0,) * len(shape), pipeline_mode=pl.Buffered(1))


def _ln_silu_mm(x, g, b, w, tm):
    m, c = x.shape
    n = w.shape[1]
    return pl.pallas_call(
        _ln_silu_mm_kernel,
        grid=(m // tm,),
        in_specs=[pl.BlockSpec((tm, c), lambda i: (i, 0)),
                  _resident((1, c)), _resident((1, c)), _resident((c, n))],
        out_specs=pl.BlockSpec((tm, n), lambda i: (i, 0)),
        out_shape=jax.ShapeDtypeStruct((m, n), F32),
        compiler_params=_params("parallel"),
        name="ln_silu_mm",
    )(x, g, b, w)


def _merge_kernel(oa_ref, ga_ref, gb_ref, oc_ref, x_ref, wa_ref, wo_ref, g_ref, b_ref, h_ref):
    br_attn = _mm(oa_ref[...], wa_ref[...])
    mixed = ga_ref[...] * br_attn + gb_ref[...] * oc_ref[...]
    y = DEEPNORM_ALPHA * x_ref[...] + _mm(mixed.astype(BF16), wo_ref[...])
    h_ref[...] = _layer_norm(y, g_ref[...], b_ref[...])


def _merge(o_attn, gates, o_conv, x, w_attn_o, w_out, g, b, tm):
    m, d = x.shape
    row = lambda i: (i, 0)
    return pl.pallas_call(
        _merge_kernel,
        grid=(m // tm,),
        in_specs=[pl.BlockSpec((tm, d), row),
                  pl.BlockSpec((tm, d), lambda i: (i, 0)),
                  pl.BlockSpec((tm, d), lambda i: (i, 1)),
                  pl.BlockSpec((tm, d), row),
                  pl.BlockSpec((tm, d), row),
                  _resident((d, d)), _resident((d, d)), _resident((1, d)), _resident((1, d))],
        out_specs=pl.BlockSpec((tm, d), row),
        out_shape=jax.ShapeDtypeStruct((m, d), F32),
        compiler_params=_params("parallel"),
        name="merge",
    )(o_attn, gates, gates, o_conv, x, w_attn_o, w_out, g, b)


PAGES_PER_STEP = 8


def _block_mean_body(page_refs, o_ref):
    for blk in range(PAGES_PER_STEP // PAGES_PER_BLOCK):
        total = None
        for p in range(PAGES_PER_BLOCK):
            part = jnp.sum(page_refs[blk * PAGES_PER_BLOCK + p][0, 0], axis=0)
            total = part if total is None else total + part
        o_ref[0, blk] = total * (1.0 / MOBA_BLOCK)


def _mlp_means_kernel(pt_ref, h_ref, w1_ref, w2_ref, g_ref, b_ref, *refs):
    page_refs = refs[:PAGES_PER_STEP]
    o_ref, kbar_ref, hb_ref, acc_ref = refs[PAGES_PER_STEP:]
    _mlp_kernel(h_ref, w1_ref, w2_ref, g_ref, b_ref, o_ref, hb_ref, acc_ref,
                side_work=functools.partial(_block_mean_body, page_refs, kbar_ref))


def _mlp_kernel(h_ref, w1_ref, w2_ref, g_ref, b_ref, o_ref, hb_ref, acc_ref, side_work=None):
    f = pl.program_id(1)

    @pl.when(f == 0)
    def _():
        hb_ref[...] = h_ref[...].astype(BF16)
        acc_ref[...] = jnp.zeros_like(acc_ref)

    a = _mm(hb_ref[...], w1_ref[...])
    a = jnp.square(jnp.maximum(a, 0.0))
    if side_work is not None:
        side_work()
    acc_ref[...] += _mm(a.astype(BF16), w2_ref[...])

    @pl.when(f == pl.num_programs(1) - 1)
    def _():
        y = DEEPNORM_ALPHA * h_ref[...] + acc_ref[...]
        o_ref[...] = _layer_norm(y, g_ref[...], b_ref[...])


def _mlp(h, w1, w2, g, b, tm, tf):
    m, d = h.shape
    ff = w1.shape[1]
    return pl.pallas_call(
        _mlp_kernel,
        grid=(m // tm, ff // tf),
        in_specs=[pl.BlockSpec((tm, d), lambda i, f: (i, 0)),
                  pl.BlockSpec((d, tf), lambda i, f: (0, f)),
                  pl.BlockSpec((tf, d), lambda i, f: (f, 0)),
                  pl.BlockSpec((1, d), lambda i, f: (0, 0)),
                  pl.BlockSpec((1, d), lambda i, f: (0, 0))],
        out_specs=pl.BlockSpec((tm, d), lambda i, f: (i, 0)),
        out_shape=jax.ShapeDtypeStruct((m, d), F32),
        scratch_shapes=[pltpu.VMEM((tm, d), BF16), pltpu.VMEM((tm, d), F32)],
        compiler_params=_params("parallel", "arbitrary"),
        name="mlp",
    )(h, w1, w2, g, b)


def _mlp_with_block_means(h, w1, w2, g, b, tm, tf, cache, page_table_flat):
    m, d = h.shape
    ff = w1.shape[1]
    n_f = ff // tf
    n_groups = page_table_flat.shape[0] // PAGES_PER_STEP
    assert (m // tm) * n_f >= n_groups
    blocks_per_step = PAGES_PER_STEP // PAGES_PER_BLOCK

    def group(i, f):
        return jnp.minimum(i * n_f + f, n_groups - 1)

    def page_spec(p):
        return pl.BlockSpec((1, 1, PAGE_SIZE, N_HEADS, HEAD_DIM),
                            lambda i, f, pt: (0, pt[group(i, f) * PAGES_PER_STEP + p], 0, 0, 0))

    y, means = pl.pallas_call(
        _mlp_means_kernel,
        grid_spec=pltpu.PrefetchScalarGridSpec(
            num_scalar_prefetch=1,
            grid=(m // tm, n_f),
            in_specs=[pl.BlockSpec((tm, d), lambda i, f, pt: (i, 0)),
                      pl.BlockSpec((d, tf), lambda i, f, pt: (0, f)),
                      pl.BlockSpec((tf, d), lambda i, f, pt: (f, 0)),
                      pl.BlockSpec((1, d), lambda i, f, pt: (0, 0)),
                      pl.BlockSpec((1, d), lambda i, f, pt: (0, 0))]
                     + [page_spec(p) for p in range(PAGES_PER_STEP)],
            out_specs=[pl.BlockSpec((tm, d), lambda i, f, pt: (i, 0)),
                       pl.BlockSpec((1, blocks_per_step, N_HEADS, HEAD_DIM),
                                    lambda i, f, pt: (group(i, f), 0, 0, 0))],
            scratch_shapes=[pltpu.VMEM((tm, d), BF16), pltpu.VMEM((tm, d), F32)]),
        out_shape=[jax.ShapeDtypeStruct((m, d), F32),
                   jax.ShapeDtypeStruct((n_groups, blocks_per_step, N_HEADS, HEAD_DIM), F32)],
        compiler_params=_params("arbitrary", "arbitrary"),
        name="mlp_block_means",
    )(page_table_flat, h, w1, w2, g, b, *([cache] * PAGES_PER_STEP))
    return y, means


def _block_mean_kernel(pt_ref, *refs):
    _block_mean_body(refs[:PAGES_PER_STEP], refs[PAGES_PER_STEP])


def _block_means(cache, page_table_flat):
    n_groups = page_table_flat.shape[0] // PAGES_PER_STEP
    blocks_per_step = PAGES_PER_STEP // PAGES_PER_BLOCK

    def page_spec(p):
        return pl.BlockSpec((1, 1, PAGE_SIZE, N_HEADS, HEAD_DIM),
                            lambda s, pt: (0, pt[s * PAGES_PER_STEP + p], 0, 0, 0))

    return pl.pallas_call(
        _block_mean_kernel,
        grid_spec=pltpu.PrefetchScalarGridSpec(
            num_scalar_prefetch=1,
            grid=(n_groups,),
            in_specs=[page_spec(p) for p in range(PAGES_PER_STEP)],
            out_specs=pl.BlockSpec((1, blocks_per_step, N_HEADS, HEAD_DIM), lambda s, pt: (s, 0, 0, 0))),
        out_shape=jax.ShapeDtypeStruct((n_groups, blocks_per_step, N_HEADS, HEAD_DIM), F32),
        compiler_params=_params("parallel"),
        name="block_means",
    )(page_table_flat, *([cache] * PAGES_PER_STEP))


def _sample_gate_kernel(q_ref, kbar_ref, o_ref):
    n_new = q_ref.shape[1]
    n_blk = kbar_ref.shape[1]
    kbar = kbar_ref[0]
    blk_row = lax.broadcasted_iota(jnp.int32, (n_blk, N_HEADS), 0)
    head_col = lax.broadcasted_iota(jnp.int32, (n_blk, N_HEADS), 1)
    for t in range(n_new):
        prod = kbar * q_ref[0, t:t + 1, :]
        scores = jnp.zeros((n_blk, N_HEADS), F32)
        for h in range(N_HEADS):
            head_score = jnp.sum(prod[:, h * HEAD_DIM:(h + 1) * HEAD_DIM], axis=1, keepdims=True)
            scores = jnp.where(head_col == h, head_score, scores)
        picks = _top3_blocks(scores, blk_row)
        for r in range(MOBA_TOPK):
            o_ref[0, t * MOBA_TOPK + r:t * MOBA_TOPK + r + 1, :] = picks[r]


def _sample_gate(q, kbar):
    nb, n_new, d = q.shape
    n_blk = kbar.shape[1]
    return pl.pallas_call(
        _sample_gate_kernel,
        grid=(nb,),
        in_specs=[pl.BlockSpec((1, n_new, d), lambda b: (b, 0, 0)),
                  pl.BlockSpec((1, n_blk, d), lambda b: (b, 0, 0))],
        out_specs=pl.BlockSpec((1, n_new * MOBA_TOPK, N_HEADS), lambda b: (b, 0, 0)),
        out_shape=jax.ShapeDtypeStruct((nb, n_new * MOBA_TOPK, N_HEADS), jnp.int32),
        compiler_params=_params("parallel"),
        name="sample_gate",
    )(q, kbar)


N_SEL_PAGES = MOBA_TOPK * PAGES_PER_BLOCK


def _page_tile(page_ref):
    return page_ref.reshape(PAGE_SIZE, HEAD_DIM)[...].astype(BF16)


def _sample_attn_kernel(sel_ref, pt_ref, slopes_ref, q_ref, ko_ref, vo_ref, *refs, past_len):
    n_new = q_ref.shape[1]
    k_pages = refs[:n_new * N_SEL_PAGES]
    v_pages = refs[n_new * N_SEL_PAGES:2 * n_new * N_SEL_PAGES]
    o_ref = refs[-1]
    b, h = pl.program_id(0), pl.program_id(1)
    slope = jnp.full((1, PAGE_SIZE), slopes_ref[h], F32)
    lane = lax.broadcasted_iota(jnp.int32, (1, PAGE_SIZE), 1)
    row = lax.broadcasted_iota(jnp.int32, (n_new, 1), 0)

    for t in range(n_new):
        q = q_ref[0, t:t + 1, :]
        q8 = jnp.broadcast_to(q, (8, HEAD_DIM)).astype(BF16)
        sel_base = ((b * N_HEADS + h) * n_new + t) * MOBA_TOPK
        logits = []
        for r in range(MOBA_TOPK):
            blk = sel_ref[sel_base + r]
            for p in range(PAGES_PER_BLOCK):
                kp = _page_tile(k_pages[(t * MOBA_TOPK + r) * PAGES_PER_BLOCK + p])
                s = lax.dot_general(q8, kp, (((1,), (1,)), ((), ())), preferred_element_type=F32)[0:1]
                dist = (past_len + t - blk * MOBA_BLOCK - p * PAGE_SIZE) - lane
                logits.append(s * ATTN_SCALE - slope * dist.astype(F32))
        s_own = jnp.sum(ko_ref[0] * q, axis=1, keepdims=True)
        l_own = s_own * ATTN_SCALE - slope[:, 0:1] * (t - row).astype(F32)
        l_own = jnp.where(row <= t, l_own, NEG_INF)

        m = jnp.max(l_own, axis=0, keepdims=True)
        for lg in logits:
            m = jnp.maximum(m, jnp.max(lg, axis=1, keepdims=True))
        p_own = jnp.exp(l_own - m)
        denom = jnp.sum(p_own, axis=0, keepdims=True)
        out = jnp.sum(p_own * vo_ref[0], axis=0, keepdims=True)
        for idx, lg in enumerate(logits):
            p = jnp.exp(lg - m)
            denom = denom + jnp.sum(p, axis=1, keepdims=True)
            p8 = jnp.broadcast_to(p, (8, PAGE_SIZE)).astype(BF16)
            out = out + _mm(p8, _page_tile(v_pages[t * N_SEL_PAGES + idx]))[0:1]
        o_ref[0, t:t + 1, :] = out / denom


def _sample_attn(sel_flat, page_table_flat, slopes, q, k_new, v_new, cache_k, cache_v, n_pages, past_len):
    nb, n_new, d = q.shape

    def page_spec(t, r, p):
        def index(b, h, sel, pt):
            blk = sel[((b * N_HEADS + h) * n_new + t) * MOBA_TOPK + r]
            return (pt[b * n_pages + blk * PAGES_PER_BLOCK + p], 0, h, 0, 0)
        return pl.BlockSpec((1, PAGE_SIZE, 1, 1, HEAD_DIM), index)

    page_specs = [page_spec(t, r, p) for t in range(n_new) for r in range(MOBA_TOPK)
                  for p in range(PAGES_PER_BLOCK)]
    head_spec = pl.BlockSpec((1, n_new, HEAD_DIM), lambda b, h, sel, pt: (b, 0, h))
    return pl.pallas_call(
        functools.partial(_sample_attn_kernel, past_len=past_len),
        grid_spec=pltpu.PrefetchScalarGridSpec(
            num_scalar_prefetch=2,
            grid=(nb, N_HEADS),
            in_specs=[pl.BlockSpec(memory_space=pltpu.SMEM), head_spec, head_spec, head_spec]
                     + page_specs + page_specs,
            out_specs=head_spec),
        out_shape=jax.ShapeDtypeStruct((nb, n_new, d), F32),
        compiler_params=_params("parallel", "parallel"),
        name="sample_attn",
    )(sel_flat, page_table_flat, slopes, q, k_new, v_new,
      *([cache_k] * len(page_specs)), *([cache_v] * len(page_specs)))


def _alibi_slopes():
    return jnp.asarray(2.0 ** (-8.0 * np.arange(1, N_HEADS + 1) / N_HEADS), F32)


def kernel(x_prompt, x_sample, cache_k, cache_v, state_conv, page_table, w_in, b_in, w_attn_o,
           w_dw, b_dw, ln_conv_g, ln_conv_b, w_conv_o, w_out, ln1_g, ln1_b, w_mlp1, w_mlp2,
           ln2_g, ln2_b):
    batch, seq, d = x_prompt.shape
    dec_batch, dec_seq, _ = x_sample.shape
    n_pages = page_table.shape[1]
    past_len = n_pages * PAGE_SIZE
    n_blk = seq // MOBA_BLOCK
    assert w_in.shape[0] == DEPTH and d == N_HEADS * HEAD_DIM
    assert seq % MOBA_BLOCK == 0 and past_len % MOBA_BLOCK == 0 and n_pages % PAGES_PER_STEP == 0

    row2 = lambda a: a[0].reshape(1, -1)
    w_in_b = w_in[0].astype(BF16)
    b_in2 = row2(b_in)
    w_attn_o_b, w_conv_o_b, w_out_b = w_attn_o[0].astype(BF16), w_conv_o[0].astype(BF16), w_out[0].astype(BF16)
    w_mlp1_b, w_mlp2_b = w_mlp1[0].astype(BF16), w_mlp2[0].astype(BF16)
    w_dw2, b_dw2 = w_dw[0], row2(b_dw)
    slopes = _alibi_slopes()
    c_q, c_k, c_v, c_a, c_g, c_gate = 0, d, 2 * d, 3 * d, 4 * d, 5 * d

    mlp_args = (w_mlp1_b, w_mlp2_b, row2(ln2_g), row2(ln2_b))
    tf = 512

    def merged(x, o_attn, gates, o_conv, tm_merge):
        return _merge(o_attn, gates, o_conv, x, w_attn_o_b, w_out_b, row2(ln1_g), row2(ln1_b), tm_merge)

    m_p = batch * seq
    xp = x_prompt.reshape(m_p, d)
    xp_b = xp.astype(BF16)
    tm, tn = 1024, 512
    qT = _proj_qT(xp_b, w_in_b, b_in2, c_q, d, tm, tn)
    k_f, k_b, kbar = _proj_k(xp_b, w_in_b, b_in2, c_k, d, tm, tn)
    v_f, vT = _proj_v(xp_b, w_in_b, b_in2, c_v, d, tm, tn)
    glu = _proj_glu(xp_b, w_in_b, b_in2, c_a, c_g, d, tm, tn)
    gates = _proj_plain(xp_b, w_in_b, b_in2, c_gate, 2 * d, tm, tn, sigmoid=True)

    o_attn = _moba_prompt(qT, k_b.reshape(m_p // MOBA_BLOCK, MOBA_BLOCK, d), vT,
                          kbar.reshape(batch, n_blk, d), slopes, batch, n_blk)
    conv = _dwconv_prompt(glu, w_dw2, b_dw2, seq, 256, 512)
    o_conv = _ln_silu_mm(conv, row2(ln_conv_g), row2(ln_conv_b), w_conv_o_b, 256)
    h_prompt = merged(xp, o_attn, gates, o_conv, 256)
    pt_flat = page_table.reshape(-1)
    tm_mlp = 512
    n_groups = pt_flat.shape[0] // PAGES_PER_STEP
    if (m_p // tm_mlp) * (w_mlp1_b.shape[1] // tf) >= n_groups:
        y_prompt, means = _mlp_with_block_means(h_prompt, *mlp_args, tm_mlp, tf, cache_k, pt_flat)
    else:
        y_prompt = _mlp(h_prompt, *mlp_args, tm_mlp, tf)
        means = _block_means(cache_k, pt_flat)
    kbar_s = means.reshape(dec_batch, past_len // MOBA_BLOCK, d)

    new_k_prompt = k_f.reshape(DEPTH, batch, seq // PAGE_SIZE, PAGE_SIZE, N_HEADS, HEAD_DIM)
    new_v_prompt = v_f.reshape(DEPTH, batch, seq // PAGE_SIZE, PAGE_SIZE, N_HEADS, HEAD_DIM)
    new_conv_prompt = glu.reshape(batch, seq, d)[:, seq - (CONV_WIDTH - 1):][None]

    m_s = dec_batch * dec_seq
    xs = x_sample.reshape(m_s, d)
    xs_b = xs.astype(BF16)
    qkv_s = _proj_plain(xs_b, w_in_b, b_in2, c_q, 3 * d, m_s, tn)
    q_s = qkv_s[:, :d].reshape(dec_batch, dec_seq, d)
    k_s = qkv_s[:, d:2 * d].reshape(dec_batch, dec_seq, d)
    v_s = qkv_s[:, 2 * d:].reshape(dec_batch, dec_seq, d)
    glu_s = _proj_glu(xs_b, w_in_b, b_in2, c_a, c_g, d, m_s, tn)
    gates_s = _proj_plain(xs_b, w_in_b, b_in2, c_gate, 2 * d, m_s, tn, sigmoid=True)

    ck = cache_k[0].reshape(cache_k.shape[1], PAGE_SIZE, N_HEADS, 1, HEAD_DIM)
    cv = cache_v[0].reshape(cache_v.shape[1], PAGE_SIZE, N_HEADS, 1, HEAD_DIM)
    sel = _sample_gate(q_s, kbar_s)
    sel_flat = sel.reshape(dec_batch, dec_seq, MOBA_TOPK, N_HEADS).transpose(0, 3, 1, 2).reshape(-1)
    o_attn_s = _sample_attn(sel_flat, pt_flat, slopes, q_s, k_s, v_s, ck, cv, n_pages, past_len)

    u_ext = jnp.concatenate([state_conv[0], glu_s.reshape(dec_batch, dec_seq, d)], axis=1)
    conv_s = _dwconv_sample(u_ext.transpose(1, 0, 2), w_dw2, b_dw2, dec_seq, 512)
    conv_s = conv_s.transpose(1, 0, 2).reshape(m_s, d)
    o_conv_s = _ln_silu_mm(conv_s, row2(ln_conv_g), row2(ln_conv_b), w_conv_o_b, m_s)
    h_sample = merged(xs, o_attn_s.reshape(m_s, d).astype(BF16), gates_s, o_conv_s, m_s)
    y_sample = _mlp(h_sample, *mlp_args, m_s, tf)

    new_k_sample = k_s.reshape(DEPTH, dec_batch, dec_seq, N_HEADS, HEAD_DIM)
    new_v_sample = v_s.reshape(DEPTH, dec_batch, dec_seq, N_HEADS, HEAD_DIM)
    new_conv_sample = u_ext[:, dec_seq:][None]

    return (y_prompt.reshape(batch, seq, d), y_sample.reshape(dec_batch, dec_seq, d),
            new_k_prompt, new_v_prompt, new_conv_prompt, new_k_sample, new_v_sample, new_conv_sample)
```

```python
import functools

import numpy as np
import jax
import jax.numpy as jnp
from jax import lax
from jax.experimental import pallas as pl
from jax.experimental.pallas import tpu as pltpu

N_HEADS = 16
HEAD_DIM = 128
MOBA_BLOCK = 256
MOBA_TOPK = 3
PAGE_SIZE = 128
PAGES_PER_BLOCK = MOBA_BLOCK // PAGE_SIZE
CONV_WIDTH = 31
CONV_HALO = 32
LN_EPS = 1e-5
DEPTH = 1
DEEPNORM_ALPHA = (2.0 * DEPTH) ** 0.25
ATTN_SCALE = HEAD_DIM ** -0.5
LOG2E = 1.4426950408889634
ATTN_TILES = 4
ATTN_QBLOCKS = 2
QW = ATTN_QBLOCKS * MOBA_BLOCK
ONES_ROWS = 16
VT_ROWS = HEAD_DIM + ONES_ROWS
LANES = 128
SUBLANES = 8
VMEM_LIMIT_BYTES = 56 * 1024 * 1024
NEG_INF = float("-inf")
BF16 = jnp.bfloat16
F32 = jnp.float32


def _params(*semantics):
    return pltpu.CompilerParams(dimension_semantics=semantics, vmem_limit_bytes=VMEM_LIMIT_BYTES)


def _sigmoid(z):
    return 1.0 / (1.0 + jnp.exp(-z))


def _layer_norm(x, g, b):
    mu = jnp.mean(x, axis=-1, keepdims=True)
    xc = x - mu
    var = jnp.mean(xc * xc, axis=-1, keepdims=True)
    return xc * lax.rsqrt(var + LN_EPS) * g + b


def _mm(a, b):
    return jnp.dot(a, b, preferred_element_type=F32)


def _proj_plain_kernel(x_ref, w_ref, b_ref, o_ref, *, sigmoid):
    z = _mm(x_ref[...], w_ref[...]) + b_ref[...]
    o_ref[...] = _sigmoid(z) if sigmoid else z


def _proj_glu_kernel(x_ref, wa_ref, wg_ref, ba_ref, bg_ref, o_ref):
    x = x_ref[...]
    a = _mm(x, wa_ref[...]) + ba_ref[...]
    g = _mm(x, wg_ref[...]) + bg_ref[...]
    o_ref[...] = a * _sigmoid(g)


def _proj_qT_kernel(x_ref, w_ref, b_ref, qT_ref):
    z = _mm(x_ref[...], w_ref[...]) + b_ref[...]
    qT_ref[...] = z.T


def _proj_k_kernel(x_ref, w_ref, b_ref, kf_ref, kb_ref, kbar_ref):
    z = _mm(x_ref[...], w_ref[...]) + b_ref[...]
    kf_ref[...] = z
    kb_ref[...] = z.astype(BF16)
    tm, tn = z.shape
    kbar_ref[0] = jnp.sum(z.reshape(tm // MOBA_BLOCK, MOBA_BLOCK, tn), axis=1) * (1.0 / MOBA_BLOCK)


def _proj_v_kernel(x_ref, w_ref, b_ref, vf_ref, vT_ref):
    z = _mm(x_ref[...], w_ref[...]) + b_ref[...]
    vf_ref[...] = z
    zt = z.T
    ones = jnp.ones((ONES_ROWS, MOBA_BLOCK), BF16)
    for c in range(vT_ref.shape[0]):
        for hh in range(z.shape[1] // HEAD_DIM):
            rows = zt[hh * HEAD_DIM:(hh + 1) * HEAD_DIM, c * MOBA_BLOCK:(c + 1) * MOBA_BLOCK]
            vT_ref[c, hh * VT_ROWS:hh * VT_ROWS + HEAD_DIM, :] = rows.astype(BF16)
            vT_ref[c, hh * VT_ROWS + HEAD_DIM:(hh + 1) * VT_ROWS, :] = ones


def _proj_specs(m, k, tm, tn, col_block):
    x_spec = pl.BlockSpec((tm, k), lambda i, j: (i, 0))
    w_spec = pl.BlockSpec((k, tn), lambda i, j: (0, col_block + j))
    b_spec = pl.BlockSpec((1, tn), lambda i, j: (0, col_block + j))
    return x_spec, w_spec, b_spec


def _proj_plain(x, w, b, col0, n, tm, tn, sigmoid=False):
    m, k = x.shape
    x_spec, w_spec, b_spec = _proj_specs(m, k, tm, tn, col0 // tn)
    return pl.pallas_call(
        functools.partial(_proj_plain_kernel, sigmoid=sigmoid),
        grid=(m // tm, n // tn),
        in_specs=[x_spec, w_spec, b_spec],
        out_specs=pl.BlockSpec((tm, tn), lambda i, j: (i, j)),
        out_shape=jax.ShapeDtypeStruct((m, n), F32),
        compiler_params=_params("parallel", "parallel"),
        name="proj_plain",
    )(x, w, b)


def _proj_glu(x, w, b, col_a, col_g, n, tm, tn):
    m, k = x.shape
    x_spec, wa_spec, ba_spec = _proj_specs(m, k, tm, tn, col_a // tn)
    _, wg_spec, bg_spec = _proj_specs(m, k, tm, tn, col_g // tn)
    return pl.pallas_call(
        _proj_glu_kernel,
        grid=(m // tm, n // tn),
        in_specs=[x_spec, wa_spec, wg_spec, ba_spec, bg_spec],
        out_specs=pl.BlockSpec((tm, tn), lambda i, j: (i, j)),
        out_shape=jax.ShapeDtypeStruct((m, n), F32),
        compiler_params=_params("parallel", "parallel"),
        name="proj_glu",
    )(x, w, w, b, b)


def _proj_qT(x, w, b, col0, n, tm, tn):
    m, k = x.shape
    x_spec, w_spec, b_spec = _proj_specs(m, k, tm, tn, col0 // tn)
    return pl.pallas_call(
        _proj_qT_kernel,
        grid=(m // tm, n // tn),
        in_specs=[x_spec, w_spec, b_spec],
        out_specs=pl.BlockSpec((tn, tm), lambda i, j: (j, i)),
        out_shape=jax.ShapeDtypeStruct((n, m), F32),
        compiler_params=_params("parallel", "parallel"),
        name="proj_qT",
    )(x, w, b)


def _proj_k(x, w, b, col0, n, tm, tn):
    m, k = x.shape
    x_spec, w_spec, b_spec = _proj_specs(m, k, tm, tn, col0 // tn)
    nb = tm // MOBA_BLOCK
    return pl.pallas_call(
        _proj_k_kernel,
        grid=(m // tm, n // tn),
        in_specs=[x_spec, w_spec, b_spec],
        out_specs=[pl.BlockSpec((tm, tn), lambda i, j: (i, j)),
                   pl.BlockSpec((tm, tn), lambda i, j: (i, j)),
                   pl.BlockSpec((1, nb, tn), lambda i, j: (i, 0, j))],
        out_shape=[jax.ShapeDtypeStruct((m, n), F32),
                   jax.ShapeDtypeStruct((m, n), BF16),
                   jax.ShapeDtypeStruct((m // tm, nb, n), F32)],
        compiler_params=_params("parallel", "parallel"),
        name="proj_k",
    )(x, w, b)


def _proj_v(x, w, b, col0, n, tm, tn):
    m, k = x.shape
    x_spec, w_spec, b_spec = _proj_specs(m, k, tm, tn, col0 // tn)
    nb = tm // MOBA_BLOCK
    return pl.pallas_call(
        _proj_v_kernel,
        grid=(m // tm, n // tn),
        in_specs=[x_spec, w_spec, b_spec],
        out_specs=[pl.BlockSpec((tm, tn), lambda i, j: (i, j)),
                   pl.BlockSpec((nb, tn // HEAD_DIM * VT_ROWS, MOBA_BLOCK), lambda i, j: (i, j, 0))],
        out_shape=[jax.ShapeDtypeStruct((m, n), F32),
                   jax.ShapeDtypeStruct((m // MOBA_BLOCK, n // HEAD_DIM * VT_ROWS, MOBA_BLOCK), BF16)],
        compiler_params=_params("parallel", "parallel"),
        name="proj_v",
    )(x, w, b)


def _top3_blocks(scores, blk_row):
    picks = []
    for _ in range(MOBA_TOPK):
        mx = jnp.max(scores, axis=0, keepdims=True)
        idx = jnp.min(jnp.where(scores == mx, blk_row, 1 << 20), axis=0, keepdims=True)
        picks.append(jnp.where(mx > NEG_INF, idx, -1))
        scores = jnp.where(blk_row == idx, NEG_INF, scores)
    return picks


def _attn_kernel(slopes_ref, qT_ref, k_ref, vT_ref, kbar_ref, o_ref, sa_ref, ta_ref, sb_ref, tb_ref):
    h = pl.program_id(1)
    i0 = pl.program_id(2) * ATTN_QBLOCKS
    n_blk = k_ref.shape[0]
    qT = qT_ref[...]
    lane = lax.broadcasted_iota(jnp.int32, (1, QW), 1)
    second = lane >= MOBA_BLOCK
    own_blk = i0 + second.astype(jnp.int32)
    gate = lax.dot_general(kbar_ref[0], qT, (((1,), (0,)), ((), ())),
                           precision=lax.Precision.HIGHEST, preferred_element_type=F32)
    blk_row = lax.broadcasted_iota(jnp.int32, (n_blk, QW), 0)
    gate = jnp.where(blk_row < own_blk, gate, NEG_INF)
    sel0, sel1, sel2 = _top3_blocks(gate, blk_row)

    qTb = (qT * (ATTN_SCALE * LOG2E)).astype(BF16)
    neg_slope2 = jnp.full((1, QW), slopes_ref[h], F32) * (-LOG2E)
    key_pos = lax.broadcasted_iota(jnp.int32, (MOBA_BLOCK, QW), 0)
    qry_pos = lax.broadcasted_iota(jnp.int32, (MOBA_BLOCK, QW), 1) & (MOBA_BLOCK - 1)
    bias = (qry_pos - key_pos).astype(F32) * neg_slope2
    causal = key_pos <= qry_pos

    def block_offset(j):
        return ((own_blk - j) * MOBA_BLOCK).astype(F32) * neg_slope2

    def picked_by(j):
        return (sel0 == j) | (sel1 == j) | (sel2 == j)

    def score_stage(step, s_ref, tmax_ref):
        for u in range(ATTN_TILES):
            j = jnp.minimum(step * ATTN_TILES + u, n_blk - 1)
            s = _mm(k_ref[j], qTb) + bias
            s_ref[u] = s
            tmax_ref[u] = jnp.max(s, axis=0, keepdims=True) + block_offset(j)

    def softmax_stage(step, s_ref, tmax_ref, m, acc):
        m_new = m
        picks = []
        for u in range(ATTN_TILES):
            j = step * ATTN_TILES + u
            picked = picked_by(jnp.where(j < i0, j, -2))
            m_new = jnp.maximum(m_new, jnp.where(picked, tmax_ref[u], NEG_INF))
            picks.append((j, picked))
        alpha = jnp.exp2(m - m_new)
        pv = None
        for u, (j, picked) in enumerate(picks):
            shift = jnp.where(picked, m_new - block_offset(j), jnp.inf)
            p = jnp.exp2(s_ref[u] - shift)
            part = _mm(vT_ref[jnp.minimum(j, n_blk - 1)], p.astype(BF16))
            pv = part if pv is None else pv + part
        return m_new, alpha * acc + pv

    w = MOBA_BLOCK
    causal1, bias1 = causal[:, :w], bias[:, :w]
    s_a = _mm(k_ref[i0], qTb) + bias
    s_a1 = jnp.where(causal1, s_a[:, :w], NEG_INF)
    s_a2 = jnp.where(picked_by(i0)[:, w:], s_a[:, w:] + block_offset(i0)[:, w:], NEG_INF)
    s_b2 = jnp.where(causal1, _mm(k_ref[i0 + 1], qTb[:, w:]) + bias1, NEG_INF)
    m1 = jnp.max(s_a1, axis=0, keepdims=True)
    m2 = jnp.maximum(jnp.max(s_a2, axis=0, keepdims=True), jnp.max(s_b2, axis=0, keepdims=True))
    m = jnp.concatenate([m1, m2], axis=1)
    p_a = jnp.concatenate([jnp.exp2(s_a1 - m1), jnp.exp2(s_a2 - m2)], axis=1).astype(BF16)
    acc = _mm(vT_ref[i0], p_a)
    acc_b2 = _mm(vT_ref[i0 + 1], jnp.exp2(s_b2 - m2).astype(BF16))
    acc = jnp.concatenate([acc[:, :w], acc[:, w:] + acc_b2], axis=1)

    score_stage(0, sa_ref, ta_ref)
    n_steps = (i0 + (ATTN_TILES - 1)) // ATTN_TILES

    def body(pair, carry):
        step = 2 * pair
        carry = softmax_stage(step, sa_ref, ta_ref, *carry)
        score_stage(step + 1, sb_ref, tb_ref)

        def second_step(c):
            c = softmax_stage(step + 1, sb_ref, tb_ref, *c)
            score_stage(step + 2, sa_ref, ta_ref)
            return c

        return lax.cond(step + 1 < n_steps, second_step, lambda c: c, carry)

    m, acc = lax.fori_loop(0, (n_steps + 1) // 2, body, (m, acc))
    out = acc[:HEAD_DIM] / acc[HEAD_DIM:HEAD_DIM + 1]
    o_ref[...] = out.T.astype(o_ref.dtype)


def _moba_prompt(qT, kb, vT, kbar, slopes, batch, n_blk):
    d, m = qT.shape
    assert n_blk % ATTN_QBLOCKS == 0 and n_blk % ATTN_TILES == 0
    n_grp = n_blk // ATTN_QBLOCKS
    return pl.pallas_call(
        _attn_kernel,
        grid=(batch, N_HEADS, n_grp),
        in_specs=[pl.BlockSpec(memory_space=pltpu.SMEM),
                  pl.BlockSpec((HEAD_DIM, QW), lambda b, h, g: (h, b * n_grp + g)),
                  pl.BlockSpec((n_blk, MOBA_BLOCK, HEAD_DIM), lambda b, h, g: (b, 0, h)),
                  pl.BlockSpec((n_blk, VT_ROWS, MOBA_BLOCK), lambda b, h, g: (b, h, 0)),
                  pl.BlockSpec((1, n_blk, HEAD_DIM), lambda b, h, g: (b, 0, h))],
        out_specs=pl.BlockSpec((QW, HEAD_DIM), lambda b, h, g: (b * n_grp + g, h)),
        out_shape=jax.ShapeDtypeStruct((m, d), BF16),
        scratch_shapes=[pltpu.VMEM((ATTN_TILES, MOBA_BLOCK, QW), F32),
                        pltpu.VMEM((ATTN_TILES, 1, QW), F32)] * 2,
        compiler_params=_params("parallel", "parallel", "parallel"),
        name="moba_prompt",
    )(slopes, qT, kb, vT, kbar)


CONV_ROW_CHUNK = 32


def _dwconv_kernel(cur_ref, prev_ref, w_ref, b_ref, o_ref, ext_ref, shift_ref, *, tiles_per_seq):
    i = pl.program_id(0)
    tm, tc = cur_ref.shape
    starts_sequence = (i % tiles_per_seq) == 0
    ext_ref[0:CONV_HALO, :] = jnp.where(starts_sequence, 0.0, prev_ref[...])
    ext_ref[CONV_HALO:, :] = cur_ref[...]
    n_shifted = tm + CONV_HALO - SUBLANES
    for phase in range(1, SUBLANES):
        shift_ref[phase - 1, 0:n_shifted, :] = ext_ref[phase:phase + n_shifted, :]
    first = CONV_HALO - (CONV_WIDTH - 1)
    for c0 in range(0, tc, LANES):
        for r0 in range(0, tm, CONV_ROW_CHUNK):
            acc = jnp.broadcast_to(b_ref[:, c0:c0 + LANES], (CONV_ROW_CHUNK, LANES))
            for k in range(CONV_WIDTH):
                aligned, phase = divmod(first + k, SUBLANES)
                src = ext_ref if phase == 0 else shift_ref.at[phase - 1]
                row = r0 + aligned * SUBLANES
                acc = acc + w_ref[k:k + 1, c0:c0 + LANES] * src[row:row + CONV_ROW_CHUNK, c0:c0 + LANES]
            o_ref[r0:r0 + CONV_ROW_CHUNK, c0:c0 + LANES] = acc


def _dwconv_prompt(u, w_dw, b_dw, seq, tm, tc):
    m, c = u.shape
    halo_per_tile = tm // CONV_HALO
    return pl.pallas_call(
        functools.partial(_dwconv_kernel, tiles_per_seq=seq // tm),
        grid=(m // tm, c // tc),
        in_specs=[pl.BlockSpec((tm, tc), lambda i, j: (i, j)),
                  pl.BlockSpec((CONV_HALO, tc), lambda i, j: (jnp.maximum(i * halo_per_tile - 1, 0), j)),
                  pl.BlockSpec((CONV_WIDTH, tc), lambda i, j: (0, j)),
                  pl.BlockSpec((1, tc), lambda i, j: (0, j))],
        out_specs=pl.BlockSpec((tm, tc), lambda i, j: (i, j)),
        out_shape=jax.ShapeDtypeStruct((m, c), F32),
        scratch_shapes=[pltpu.VMEM((tm + CONV_HALO, tc), F32),
                        pltpu.VMEM((SUBLANES - 1, tm + CONV_HALO - SUBLANES, tc), F32)],
        compiler_params=_params("parallel", "parallel"),
        name="dwconv_prompt",
    )(u, u, w_dw, b_dw)


def _dwconv_sample_kernel(u_ref, w_ref, b_ref, o_ref):
    for t in range(o_ref.shape[0]):
        acc = jnp.broadcast_to(b_ref[...], o_ref.shape[1:])
        for k in range(CONV_WIDTH):
            acc = acc + w_ref[k:k + 1, :] * u_ref[t + k]
        o_ref[t] = acc


def _dwconv_sample(u_tmajor, w_dw, b_dw, n_new, tc):
    t_ext, nb, c = u_tmajor.shape
    return pl.pallas_call(
        _dwconv_sample_kernel,
        grid=(c // tc,),
        in_specs=[pl.BlockSpec((t_ext, nb, tc), lambda j: (0, 0, j)),
                  pl.BlockSpec((CONV_WIDTH, tc), lambda j: (0, j)),
                  pl.BlockSpec((1, tc), lambda j: (0, j))],
        out_specs=pl.BlockSpec((n_new, nb, tc), lambda j: (0, 0, j)),
        out_shape=jax.ShapeDtypeStruct((n_new, nb, c), F32),
        compiler_params=_params("parallel"),
        name="dwconv_sample",
    )(u_tmajor, w_dw, b_dw)


def _ln_silu_mm_kernel(x_ref, g_ref, b_ref, w_ref, o_ref):
    y = _layer_norm(x_ref[...], g_ref[...], b_ref[...])
    hidden = y * _sigmoid(y)
    o_ref[...] = _mm(hidden.astype(BF16), w_ref[...])


def _resident(shape):
    return pl.BlockSpec(shape, lambda *_: (0,) * len(shape), pipeline_mode=pl.Buffered(1))


def _ln_silu_mm(x, g, b, w, tm):
    m, c = x.shape
    n = w.shape[1]
    return pl.pallas_call(
        _ln_silu_mm_kernel,
        grid=(m // tm,),
        in_specs=[pl.BlockSpec((tm, c), lambda i: (i, 0)),
                  _resident((1, c)), _resident((1, c)), _resident((c, n))],
        out_specs=pl.BlockSpec((tm, n), lambda i: (i, 0)),
        out_shape=jax.ShapeDtypeStruct((m, n), F32),
        compiler_params=_params("parallel"),
        name="ln_silu_mm",
    )(x, g, b, w)


def _merge_kernel(oa_ref, ga_ref, gb_ref, oc_ref, x_ref, wa_ref, wo_ref, g_ref, b_ref, h_ref):
    br_attn = _mm(oa_ref[...], wa_ref[...])
    mixed = ga_ref[...] * br_attn + gb_ref[...] * oc_ref[...]
    y = DEEPNORM_ALPHA * x_ref[...] + _mm(mixed.astype(BF16), wo_ref[...])
    h_ref[...] = _layer_norm(y, g_ref[...], b_ref[...])


def _merge(o_attn, gates, o_conv, x, w_attn_o, w_out, g, b, tm):
    m, d = x.shape
    row = lambda i: (i, 0)
    return pl.pallas_call(
        _merge_kernel,
        grid=(m // tm,),
        in_specs=[pl.BlockSpec((tm, d), row),
                  pl.BlockSpec((tm, d), lambda i: (i, 0)),
                  pl.BlockSpec((tm, d), lambda i: (i, 1)),
                  pl.BlockSpec((tm, d), row),
                  pl.BlockSpec((tm, d), row),
                  _resident((d, d)), _resident((d, d)), _resident((1, d)), _resident((1, d))],
        out_specs=pl.BlockSpec((tm, d), row),
        out_shape=jax.ShapeDtypeStruct((m, d), F32),
        compiler_params=_params("parallel"),
        name="merge",
    )(o_attn, gates, gates, o_conv, x, w_attn_o, w_out, g, b)


PAGES_PER_STEP = 8


def _block_mean_body(page_refs, o_ref):
    for blk in range(PAGES_PER_STEP // PAGES_PER_BLOCK):
        total = None
        for p in range(PAGES_PER_BLOCK):
            part = jnp.sum(page_refs[blk * PAGES_PER_BLOCK + p][0, 0], axis=0)
            total = part if total is None else total + part
        o_ref[0, blk] = total * (1.0 / MOBA_BLOCK)


def _mlp_means_kernel(pt_ref, h_ref, w1_ref, w2_ref, g_ref, b_ref, *refs):
    page_refs = refs[:PAGES_PER_STEP]
    o_ref, kbar_ref, hb_ref, acc_ref = refs[PAGES_PER_STEP:]
    _mlp_kernel(h_ref, w1_ref, w2_ref, g_ref, b_ref, o_ref, hb_ref, acc_ref,
                side_work=functools.partial(_block_mean_body, page_refs, kbar_ref))


def _mlp_kernel(h_ref, w1_ref, w2_ref, g_ref, b_ref, o_ref, hb_ref, acc_ref, side_work=None):
    f = pl.program_id(1)

    @pl.when(f == 0)
    def _():
        hb_ref[...] = h_ref[...].astype(BF16)
        acc_ref[...] = jnp.zeros_like(acc_ref)

    a = _mm(hb_ref[...], w1_ref[...])
    a = jnp.square(jnp.maximum(a, 0.0))
    if side_work is not None:
        side_work()
    acc_ref[...] += _mm(a.astype(BF16), w2_ref[...])

    @pl.when(f == pl.num_programs(1) - 1)
    def _():
        y = DEEPNORM_ALPHA * h_ref[...] + acc_ref[...]
        o_ref[...] = _layer_norm(y, g_ref[...], b_ref[...])


def _mlp(h, w1, w2, g, b, tm, tf):
    m, d = h.shape
    ff = w1.shape[1]
    return pl.pallas_call(
        _mlp_kernel,
        grid=(m // tm, ff // tf),
        in_specs=[pl.BlockSpec((tm, d), lambda i, f: (i, 0)),
                  pl.BlockSpec((d, tf), lambda i, f: (0, f)),
                  pl.BlockSpec((tf, d), lambda i, f: (f, 0)),
                  pl.BlockSpec((1, d), lambda i, f: (0, 0)),
                  pl.BlockSpec((1, d), lambda i, f: (0, 0))],
        out_specs=pl.BlockSpec((tm, d), lambda i, f: (i, 0)),
        out_shape=jax.ShapeDtypeStruct((m, d), F32),
        scratch_shapes=[pltpu.VMEM((tm, d), BF16), pltpu.VMEM((tm, d), F32)],
        compiler_params=_params("parallel", "arbitrary"),
        name="mlp",
    )(h, w1, w2, g, b)


def _mlp_with_block_means(h, w1, w2, g, b, tm, tf, cache, page_table_flat):
    m, d = h.shape
    ff = w1.shape[1]
    n_f = ff // tf
    n_groups = page_table_flat.shape[0] // PAGES_PER_STEP
    assert (m // tm) * n_f >= n_groups
    blocks_per_step = PAGES_PER_STEP // PAGES_PER_BLOCK

    def group(i, f):
        return jnp.minimum(i * n_f + f, n_groups - 1)

    def page_spec(p):
        return pl.BlockSpec((1, 1, PAGE_SIZE, N_HEADS, HEAD_DIM),
                            lambda i, f, pt: (0, pt[group(i, f) * PAGES_PER_STEP + p], 0, 0, 0))

    y, means = pl.pallas_call(
        _mlp_means_kernel,
        grid_spec=pltpu.PrefetchScalarGridSpec(
            num_scalar_prefetch=1,
            grid=(m // tm, n_f),
            in_specs=[pl.BlockSpec((tm, d), lambda i, f, pt: (i, 0)),
                      pl.BlockSpec((d, tf), lambda i, f, pt: (0, f)),
                      pl.BlockSpec((tf, d), lambda i, f, pt: (f, 0)),
                      pl.BlockSpec((1, d), lambda i, f, pt: (0, 0)),
                      pl.BlockSpec((1, d), lambda i, f, pt: (0, 0))]
                     + [page_spec(p) for p in range(PAGES_PER_STEP)],
            out_specs=[pl.BlockSpec((tm, d), lambda i, f, pt: (i, 0)),
                       pl.BlockSpec((1, blocks_per_step, N_HEADS, HEAD_DIM),
                                    lambda i, f, pt: (group(i, f), 0, 0, 0))],
            scratch_shapes=[pltpu.VMEM((tm, d), BF16), pltpu.VMEM((tm, d), F32)]),
        out_shape=[jax.ShapeDtypeStruct((m, d), F32),
                   jax.ShapeDtypeStruct((n_groups, blocks_per_step, N_HEADS, HEAD_DIM), F32)],
        compiler_params=_params("arbitrary", "arbitrary"),
        name="mlp_block_means",
    )(page_table_flat, h, w1, w2, g, b, *([cache] * PAGES_PER_STEP))
    return y, means


def _block_mean_kernel(pt_ref, *refs):
    _block_mean_body(refs[:PAGES_PER_STEP], refs[PAGES_PER_STEP])


def _block_means(cache, page_table_flat):
    n_groups = page_table_flat.shape[0] // PAGES_PER_STEP
    blocks_per_step = PAGES_PER_STEP // PAGES_PER_BLOCK

    def page_spec(p):
        return pl.BlockSpec((1, 1, PAGE_SIZE, N_HEADS, HEAD_DIM),
                            lambda s, pt: (0, pt[s * PAGES_PER_STEP + p], 0, 0, 0))

    return pl.pallas_call(
        _block_mean_kernel,
        grid_spec=pltpu.PrefetchScalarGridSpec(
            num_scalar_prefetch=1,
            grid=(n_groups,),
            in_specs=[page_spec(p) for p in range(PAGES_PER_STEP)],
            out_specs=pl.BlockSpec((1, blocks_per_step, N_HEADS, HEAD_DIM), lambda s, pt: (s, 0, 0, 0))),
        out_shape=jax.ShapeDtypeStruct((n_groups, blocks_per_step, N_HEADS, HEAD_DIM), F32),
        compiler_params=_params("parallel"),
        name="block_means",
    )(page_table_flat, *([cache] * PAGES_PER_STEP))


def _sample_gate_kernel(q_ref, kbar_ref, o_ref):
    n_new = q_ref.shape[1]
    n_blk = kbar_ref.shape[1]
    kbar = kbar_ref[0]
    blk_row = lax.broadcasted_iota(jnp.int32, (n_blk, N_HEADS), 0)
    head_col = lax.broadcasted_iota(jnp.int32, (n_blk, N_HEADS), 1)
    for t in range(n_new):
        prod = kbar * q_ref[0, t:t + 1, :]
        scores = jnp.zeros((n_blk, N_HEADS), F32)
        for h in range(N_HEADS):
            head_score = jnp.sum(prod[:, h * HEAD_DIM:(h + 1) * HEAD_DIM], axis=1, keepdims=True)
            scores = jnp.where(head_col == h, head_score, scores)
        picks = _top3_blocks(scores, blk_row)
        for r in range(MOBA_TOPK):
            o_ref[0, t * MOBA_TOPK + r:t * MOBA_TOPK + r + 1, :] = picks[r]


def _sample_gate(q, kbar):
    nb, n_new, d = q.shape
    n_blk = kbar.shape[1]
    return pl.pallas_call(
        _sample_gate_kernel,
        grid=(nb,),
        in_specs=[pl.BlockSpec((1, n_new, d), lambda b: (b, 0, 0)),
                  pl.BlockSpec((1, n_blk, d), lambda b: (b, 0, 0))],
        out_specs=pl.BlockSpec((1, n_new * MOBA_TOPK, N_HEADS), lambda b: (b, 0, 0)),
        out_shape=jax.ShapeDtypeStruct((nb, n_new * MOBA_TOPK, N_HEADS), jnp.int32),
        compiler_params=_params("parallel"),
        name="sample_gate",
    )(q, kbar)


N_SEL_PAGES = MOBA_TOPK * PAGES_PER_BLOCK


def _page_tile(page_ref):
    return page_ref.reshape(PAGE_SIZE, HEAD_DIM)[...].astype(BF16)


def _sample_attn_kernel(sel_ref, pt_ref, slopes_ref, q_ref, ko_ref, vo_ref, *refs, past_len):
    n_new = q_ref.shape[1]
    k_pages = refs[:n_new * N_SEL_PAGES]
    v_pages = refs[n_new * N_SEL_PAGES:2 * n_new * N_SEL_PAGES]
    o_ref = refs[-1]
    b, h = pl.program_id(0), pl.program_id(1)
    slope = jnp.full((1, PAGE_SIZE), slopes_ref[h], F32)
    lane = lax.broadcasted_iota(jnp.int32, (1, PAGE_SIZE), 1)
    row = lax.broadcasted_iota(jnp.int32, (n_new, 1), 0)

    for t in range(n_new):
        q = q_ref[0, t:t + 1, :]
        q8 = jnp.broadcast_to(q, (8, HEAD_DIM)).astype(BF16)
        sel_base = ((b * N_HEADS + h) * n_new + t) * MOBA_TOPK
        logits = []
        for r in range(MOBA_TOPK):
            blk = sel_ref[sel_base + r]
            for p in range(PAGES_PER_BLOCK):
                kp = _page_tile(k_pages[(t * MOBA_TOPK + r) * PAGES_PER_BLOCK + p])
                s = lax.dot_general(q8, kp, (((1,), (1,)), ((), ())), preferred_element_type=F32)[0:1]
                dist = (past_len + t - blk * MOBA_BLOCK - p * PAGE_SIZE) - lane
                logits.append(s * ATTN_SCALE - slope * dist.astype(F32))
        s_own = jnp.sum(ko_ref[0] * q, axis=1, keepdims=True)
        l_own = s_own * ATTN_SCALE - slope[:, 0:1] * (t - row).astype(F32)
        l_own = jnp.where(row <= t, l_own, NEG_INF)

        m = jnp.max(l_own, axis=0, keepdims=True)
        for lg in logits:
            m = jnp.maximum(m, jnp.max(lg, axis=1, keepdims=True))
        p_own = jnp.exp(l_own - m)
        denom = jnp.sum(p_own, axis=0, keepdims=True)
        out = jnp.sum(p_own * vo_ref[0], axis=0, keepdims=True)
        for idx, lg in enumerate(logits):
            p = jnp.exp(lg - m)
            denom = denom + jnp.sum(p, axis=1, keepdims=True)
            p8 = jnp.broadcast_to(p, (8, PAGE_SIZE)).astype(BF16)
            out = out + _mm(p8, _page_tile(v_pages[t * N_SEL_PAGES + idx]))[0:1]
        o_ref[0, t:t + 1, :] = out / denom


def _sample_attn(sel_flat, page_table_flat, slopes, q, k_new, v_new, cache_k, cache_v, n_pages, past_len):
    nb, n_new, d = q.shape

    def page_spec(t, r, p):
        def index(b, h, sel, pt):
            blk = sel[((b * N_HEADS + h) * n_new + t) * MOBA_TOPK + r]
            return (pt[b * n_pages + blk * PAGES_PER_BLOCK + p], 0, h, 0, 0)
        return pl.BlockSpec((1, PAGE_SIZE, 1, 1, HEAD_DIM), index)

    page_specs = [page_spec(t, r, p) for t in range(n_new) for r in range(MOBA_TOPK)
                  for p in range(PAGES_PER_BLOCK)]
    head_spec = pl.BlockSpec((1, n_new, HEAD_DIM), lambda b, h, sel, pt: (b, 0, h))
    return pl.pallas_call(
        functools.partial(_sample_attn_kernel, past_len=past_len),
        grid_spec=pltpu.PrefetchScalarGridSpec(
            num_scalar_prefetch=2,
            grid=(nb, N_HEADS),
            in_specs=[pl.BlockSpec(memory_space=pltpu.SMEM), head_spec, head_spec, head_spec]
                     + page_specs + page_specs,
            out_specs=head_spec),
        out_shape=jax.ShapeDtypeStruct((nb, n_new, d), F32),
        compiler_params=_params("parallel", "parallel"),
        name="sample_attn",
    )(sel_flat, page_table_flat, slopes, q, k_new, v_new,
      *([cache_k] * len(page_specs)), *([cache_v] * len(page_specs)))


def _alibi_slopes():
    return jnp.asarray(2.0 ** (-8.0 * np.arange(1, N_HEADS + 1) / N_HEADS), F32)


def kernel(x_prompt, x_sample, cache_k, cache_v, state_conv, page_table, w_in, b_in, w_attn_o,
           w_dw, b_dw, ln_conv_g, ln_conv_b, w_conv_o, w_out, ln1_g, ln1_b, w_mlp1, w_mlp2,
           ln2_g, ln2_b):
    batch, seq, d = x_prompt.shape
    dec_batch, dec_seq, _ = x_sample.shape
    n_pages = page_table.shape[1]
    past_len = n_pages * PAGE_SIZE
    n_blk = seq // MOBA_BLOCK
    assert w_in.shape[0] == DEPTH and d == N_HEADS * HEAD_DIM
    assert seq % MOBA_BLOCK == 0 and past_len % MOBA_BLOCK == 0 and n_pages % PAGES_PER_STEP == 0

    row2 = lambda a: a[0].reshape(1, -1)
    w_in_b = w_in[0].astype(BF16)
    b_in2 = row2(b_in)
    w_attn_o_b, w_conv_o_b, w_out_b = w_attn_o[0].astype(BF16), w_conv_o[0].astype(BF16), w_out[0].astype(BF16)
    w_mlp1_b, w_mlp2_b = w_mlp1[0].astype(BF16), w_mlp2[0].astype(BF16)
    w_dw2, b_dw2 = w_dw[0], row2(b_dw)
    slopes = _alibi_slopes()
    c_q, c_k, c_v, c_a, c_g, c_gate = 0, d, 2 * d, 3 * d, 4 * d, 5 * d

    mlp_args = (w_mlp1_b, w_mlp2_b, row2(ln2_g), row2(ln2_b))
    tf = 512

    def merged(x, o_attn, gates, o_conv, tm_merge):
        return _merge(o_attn, gates, o_conv, x, w_attn_o_b, w_out_b, row2(ln1_g), row2(ln1_b), tm_merge)

    m_p = batch * seq
    xp = x_prompt.reshape(m_p, d)
    xp_b = xp.astype(BF16)
    tm, tn = 1024, 512
    qT = _proj_qT(xp_b, w_in_b, b_in2, c_q, d, tm, tn)
    k_f, k_b, kbar = _proj_k(xp_b, w_in_b, b_in2, c_k, d, tm, tn)
    v_f, vT = _proj_v(xp_b, w_in_b, b_in2, c_v, d, tm, tn)
    glu = _proj_glu(xp_b, w_in_b, b_in2, c_a, c_g, d, tm, tn)
    gates = _proj_plain(xp_b, w_in_b, b_in2, c_gate, 2 * d, tm, tn, sigmoid=True)

    o_attn = _moba_prompt(qT, k_b.reshape(m_p // MOBA_BLOCK, MOBA_BLOCK, d), vT,
                          kbar.reshape(batch, n_blk, d), slopes, batch, n_blk)
    conv = _dwconv_prompt(glu, w_dw2, b_dw2, seq, 256, 512)
    o_conv = _ln_silu_mm(conv, row2(ln_conv_g), row2(ln_conv_b), w_conv_o_b, 256)
    h_prompt = merged(xp, o_attn, gates, o_conv, 256)
    pt_flat = page_table.reshape(-1)
    tm_mlp = 512
    n_groups = pt_flat.shape[0] // PAGES_PER_STEP
    if (m_p // tm_mlp) * (w_mlp1_b.shape[1] // tf) >= n_groups:
        y_prompt, means = _mlp_with_block_means(h_prompt, *mlp_args, tm_mlp, tf, cache_k, pt_flat)
    else:
        y_prompt = _mlp(h_prompt, *mlp_args, tm_mlp, tf)
        means = _block_means(cache_k, pt_flat)
    kbar_s = means.reshape(dec_batch, past_len // MOBA_BLOCK, d)

    new_k_prompt = k_f.reshape(DEPTH, batch, seq // PAGE_SIZE, PAGE_SIZE, N_HEADS, HEAD_DIM)
    new_v_prompt = v_f.reshape(DEPTH, batch, seq // PAGE_SIZE, PAGE_SIZE, N_HEADS, HEAD_DIM)
    new_conv_prompt = glu.reshape(batch, seq, d)[:, seq - (CONV_WIDTH - 1):][None]

    m_s = dec_batch * dec_seq
    xs = x_sample.reshape(m_s, d)
    xs_b = xs.astype(BF16)
    qkv_s = _proj_plain(xs_b, w_in_b, b_in2, c_q, 3 * d, m_s, tn)
    q_s = qkv_s[:, :d].reshape(dec_batch, dec_seq, d)
    k_s = qkv_s[:, d:2 * d].reshape(dec_batch, dec_seq, d)
    v_s = qkv_s[:, 2 * d:].reshape(dec_batch, dec_seq, d)
    glu_s = _proj_glu(xs_b, w_in_b, b_in2, c_a, c_g, d, m_s, tn)
    gates_s = _proj_plain(xs_b, w_in_b, b_in2, c_gate, 2 * d, m_s, tn, sigmoid=True)

    ck = cache_k[0].reshape(cache_k.shape[1], PAGE_SIZE, N_HEADS, 1, HEAD_DIM)
    cv = cache_v[0].reshape(cache_v.shape[1], PAGE_SIZE, N_HEADS, 1, HEAD_DIM)
    sel = _sample_gate(q_s, kbar_s)
    sel_flat = sel.reshape(dec_batch, dec_seq, MOBA_TOPK, N_HEADS).transpose(0, 3, 1, 2).reshape(-1)
    o_attn_s = _sample_attn(sel_flat, pt_flat, slopes, q_s, k_s, v_s, ck, cv, n_pages, past_len)

    u_ext = jnp.concatenate([state_conv[0], glu_s.reshape(dec_batch, dec_seq, d)], axis=1)
    conv_s = _dwconv_sample(u_ext.transpose(1, 0, 2), w_dw2, b_dw2, dec_seq, 512)
    conv_s = conv_s.transpose(1, 0, 2).reshape(m_s, d)
    o_conv_s = _ln_silu_mm(conv_s, row2(ln_conv_g), row2(ln_conv_b), w_conv_o_b, m_s)
    h_sample = merged(xs, o_attn_s.reshape(m_s, d).astype(BF16), gates_s, o_conv_s, m_s)
    y_sample = _mlp(h_sample, *mlp_args, m_s, tf)

    new_k_sample = k_s.reshape(DEPTH, dec_batch, dec_seq, N_HEADS, HEAD_DIM)
    new_v_sample = v_s.reshape(DEPTH, dec_batch, dec_seq, N_HEADS, HEAD_DIM)
    new_conv_sample = u_ext[:, dec_seq:][None]

    return (y_prompt.reshape(batch, seq, d), y_sample.reshape(dec_batch, dec_seq, d),
            new_k_prompt, new_v_prompt, new_conv_prompt, new_k_sample, new_v_sample, new_conv_sample)
```

```python
import functools

import numpy as np
import jax
import jax.numpy as jnp
from jax import lax
from jax.experimental import pallas as pl
from jax.experimental.pallas import tpu as pltpu

N_HEADS = 16
HEAD_DIM = 128
MOBA_BLOCK = 256
MOBA_TOPK = 3
PAGE_SIZE = 128
PAGES_PER_BLOCK = MOBA_BLOCK // PAGE_SIZE
CONV_WIDTH = 31
CONV_HALO = 32
LN_EPS = 1e-5
DEPTH = 1
DEEPNORM_ALPHA = (2.0 * DEPTH) ** 0.25
ATTN_SCALE = HEAD_DIM ** -0.5
LOG2E = 1.4426950408889634
ATTN_TILES = 4
ATTN_QBLOCKS = 2
QW = ATTN_QBLOCKS * MOBA_BLOCK
ONES_ROWS = 16
VT_ROWS = HEAD_DIM + ONES_ROWS
LANES = 128
SUBLANES = 8
VMEM_LIMIT_BYTES = 56 * 1024 * 1024
NEG_INF = float("-inf")
BF16 = jnp.bfloat16
F32 = jnp.float32


def _params(*semantics):
    return pltpu.CompilerParams(dimension_semantics=semantics, vmem_limit_bytes=VMEM_LIMIT_BYTES)


def _sigmoid(z):
    return 1.0 / (1.0 + jnp.exp(-z))


def _layer_norm(x, g, b):
    mu = jnp.mean(x, axis=-1, keepdims=True)
    xc = x - mu
    var = jnp.mean(xc * xc, axis=-1, keepdims=True)
    return xc * lax.rsqrt(var + LN_EPS) * g + b


def _mm(a, b):
    return jnp.dot(a, b, preferred_element_type=F32)


def _proj_plain_kernel(x_ref, w_ref, b_ref, o_ref, *, sigmoid):
    z = _mm(x_ref[...], w_ref[...]) + b_ref[...]
    o_ref[...] = _sigmoid(z) if sigmoid else z


def _proj_glu_kernel(x_ref, wa_ref, wg_ref, ba_ref, bg_ref, o_ref):
    x = x_ref[...]
    a = _mm(x, wa_ref[...]) + ba_ref[...]
    g = _mm(x, wg_ref[...]) + bg_ref[...]
    o_ref[...] = a * _sigmoid(g)


def _proj_qT_kernel(x_ref, w_ref, b_ref, qT_ref):
    z = _mm(x_ref[...], w_ref[...]) + b_ref[...]
    qT_ref[...] = z.T


def _proj_k_kernel(x_ref, w_ref, b_ref, kf_ref, kb_ref, kbar_ref):
    z = _mm(x_ref[...], w_ref[...]) + b_ref[...]
    kf_ref[...] = z
    kb_ref[...] = z.astype(BF16)
    tm, tn = z.shape
    kbar_ref[0] = jnp.sum(z.reshape(tm // MOBA_BLOCK, MOBA_BLOCK, tn), axis=1) * (1.0 / MOBA_BLOCK)


def _proj_v_kernel(x_ref, w_ref, b_ref, vf_ref, vT_ref):
    z = _mm(x_ref[...], w_ref[...]) + b_ref[...]
    vf_ref[...] = z
    zt = z.T
    ones = jnp.ones((ONES_ROWS, MOBA_BLOCK), BF16)
    for c in range(vT_ref.shape[0]):
        for hh in range(z.shape[1] // HEAD_DIM):
            rows = zt[hh * HEAD_DIM:(hh + 1) * HEAD_DIM, c * MOBA_BLOCK:(c + 1) * MOBA_BLOCK]
            vT_ref[c, hh * VT_ROWS:hh * VT_ROWS + HEAD_DIM, :] = rows.astype(BF16)
            vT_ref[c, hh * VT_ROWS + HEAD_DIM:(hh + 1) * VT_ROWS, :] = ones


def _proj_specs(m, k, tm, tn, col_block):
    x_spec = pl.BlockSpec((tm, k), lambda i, j: (i, 0))
    w_spec = pl.BlockSpec((k, tn), lambda i, j: (0, col_block + j))
    b_spec = pl.BlockSpec((1, tn), lambda i, j: (0, col_block + j))
    return x_spec, w_spec, b_spec


def _proj_plain(x, w, b, col0, n, tm, tn, sigmoid=False):
    m, k = x.shape
    x_spec, w_spec, b_spec = _proj_specs(m, k, tm, tn, col0 // tn)
    return pl.pallas_call(
        functools.partial(_proj_plain_kernel, sigmoid=sigmoid),
        grid=(m // tm, n // tn),
        in_specs=[x_spec, w_spec, b_spec],
        out_specs=pl.BlockSpec((tm, tn), lambda i, j: (i, j)),
        out_shape=jax.ShapeDtypeStruct((m, n), F32),
        compiler_params=_params("parallel", "parallel"),
        name="proj_plain",
    )(x, w, b)


def _proj_glu(x, w, b, col_a, col_g, n, tm, tn):
    m, k = x.shape
    x_spec, wa_spec, ba_spec = _proj_specs(m, k, tm, tn, col_a // tn)
    _, wg_spec, bg_spec = _proj_specs(m, k, tm, tn, col_g // tn)
    return pl.pallas_call(
        _proj_glu_kernel,
        grid=(m // tm, n // tn),
        in_specs=[x_spec, wa_spec, wg_spec, ba_spec, bg_spec],
        out_specs=pl.BlockSpec((tm, tn), lambda i, j: (i, j)),
        out_shape=jax.ShapeDtypeStruct((m, n), F32),
        compiler_params=_params("parallel", "parallel"),
        name="proj_glu",
    )(x, w, w, b, b)


def _proj_qT(x, w, b, col0, n, tm, tn):
    m, k = x.shape
    x_spec, w_spec, b_spec = _proj_specs(m, k, tm, tn, col0 // tn)
    return pl.pallas_call(
        _proj_qT_kernel,
        grid=(m // tm, n // tn),
        in_specs=[x_spec, w_spec, b_spec],
        out_specs=pl.BlockSpec((tn, tm), lambda i, j: (j, i)),
        out_shape=jax.ShapeDtypeStruct((n, m), F32),
        compiler_params=_params("parallel", "parallel"),
        name="proj_qT",
    )(x, w, b)


def _proj_k(x, w, b, col0, n, tm, tn):
    m, k = x.shape
    x_spec, w_spec, b_spec = _proj_specs(m, k, tm, tn, col0 // tn)
    nb = tm // MOBA_BLOCK
    return pl.pallas_call(
        _proj_k_kernel,
        grid=(m // tm, n // tn),
        in_specs=[x_spec, w_spec, b_spec],
        out_specs=[pl.BlockSpec((tm, tn), lambda i, j: (i, j)),
                   pl.BlockSpec((tm, tn), lambda i, j: (i, j)),
                   pl.BlockSpec((1, nb, tn), lambda i, j: (i, 0, j))],
        out_shape=[jax.ShapeDtypeStruct((m, n), F32),
                   jax.ShapeDtypeStruct((m, n), BF16),
                   jax.ShapeDtypeStruct((m // tm, nb, n), F32)],
        compiler_params=_params("parallel", "parallel"),
        name="proj_k",
    )(x, w, b)


def _proj_v(x, w, b, col0, n, tm, tn):
    m, k = x.shape
    x_spec, w_spec, b_spec = _proj_specs(m, k, tm, tn, col0 // tn)
    nb = tm // MOBA_BLOCK
    return pl.pallas_call(
        _proj_v_kernel,
        grid=(m // tm, n // tn),
        in_specs=[x_spec, w_spec, b_spec],
        out_specs=[pl.BlockSpec((tm, tn), lambda i, j: (i, j)),
                   pl.BlockSpec((nb, tn // HEAD_DIM * VT_ROWS, MOBA_BLOCK), lambda i, j: (i, j, 0))],
        out_shape=[jax.ShapeDtypeStruct((m, n), F32),
                   jax.ShapeDtypeStruct((m // MOBA_BLOCK, n // HEAD_DIM * VT_ROWS, MOBA_BLOCK), BF16)],
        compiler_params=_params("parallel", "parallel"),
        name="proj_v",
    )(x, w, b)


def _top3_blocks(scores, blk_row):
    picks = []
    for _ in range(MOBA_TOPK):
        mx = jnp.max(scores, axis=0, keepdims=True)
        idx = jnp.min(jnp.where(scores == mx, blk_row, 1 << 20), axis=0, keepdims=True)
        picks.append(jnp.where(mx > NEG_INF, idx, -1))
        scores = jnp.where(blk_row == idx, NEG_INF, scores)
    return picks


def _attn_kernel(slopes_ref, qT_ref, k_ref, vT_ref, kbar_ref, o_ref, sa_ref, ta_ref, sb_ref, tb_ref):
    h = pl.program_id(1)
    i0 = pl.program_id(2) * ATTN_QBLOCKS
    n_blk = k_ref.shape[0]
    qT = qT_ref[...]
    lane = lax.broadcasted_iota(jnp.int32, (1, QW), 1)
    second = lane >= MOBA_BLOCK
    own_blk = i0 + second.astype(jnp.int32)
    gate = lax.dot_general(kbar_ref[0], qT, (((1,), (0,)), ((), ())),
                           precision=lax.Precision.HIGHEST, preferred_element_type=F32)
    blk_row = lax.broadcasted_iota(jnp.int32, (n_blk, QW), 0)
    gate = jnp.where(blk_row < own_blk, gate, NEG_INF)
    sel0, sel1, sel2 = _top3_blocks(gate, blk_row)

    qTb = (qT * (ATTN_SCALE * LOG2E)).astype(BF16)
    neg_slope2 = jnp.full((1, QW), slopes_ref[h], F32) * (-LOG2E)
    key_pos = lax.broadcasted_iota(jnp.int32, (MOBA_BLOCK, QW), 0)
    qry_pos = lax.broadcasted_iota(jnp.int32, (MOBA_BLOCK, QW), 1) & (MOBA_BLOCK - 1)
    bias = (qry_pos - key_pos).astype(F32) * neg_slope2
    causal = key_pos <= qry_pos

    def block_offset(j):
        return ((own_blk - j) * MOBA_BLOCK).astype(F32) * neg_slope2

    def picked_by(j):
        return (sel0 == j) | (sel1 == j) | (sel2 == j)

    def score_stage(step, s_ref, tmax_ref):
        for u in range(ATTN_TILES):
            j = jnp.minimum(step * ATTN_TILES + u, n_blk - 1)
            s = _mm(k_ref[j], qTb) + bias
            s_ref[u] = s
            tmax_ref[u] = jnp.max(s, axis=0, keepdims=True) + block_offset(j)

    def softmax_stage(step, s_ref, tmax_ref, m, acc):
        m_new = m
        picks = []
        for u in range(ATTN_TILES):
            j = step * ATTN_TILES + u
            picked = picked_by(jnp.where(j < i0, j, -2))
            m_new = jnp.maximum(m_new, jnp.where(picked, tmax_ref[u], NEG_INF))
            picks.append((j, picked))
        alpha = jnp.exp2(m - m_new)
        pv = None
        for u, (j, picked) in enumerate(picks):
            shift = jnp.where(picked, m_new - block_offset(j), jnp.inf)
            p = jnp.exp2(s_ref[u] - shift)
            part = _mm(vT_ref[jnp.minimum(j, n_blk - 1)], p.astype(BF16))
            pv = part if pv is None else pv + part
        return m_new, alpha * acc + pv

    w = MOBA_BLOCK
    causal1, bias1 = causal[:, :w], bias[:, :w]
    s_a = _mm(k_ref[i0], qTb) + bias
    s_a1 = jnp.where(causal1, s_a[:, :w], NEG_INF)
    s_a2 = jnp.where(picked_by(i0)[:, w:], s_a[:, w:] + block_offset(i0)[:, w:], NEG_INF)
    s_b2 = jnp.where(causal1, _mm(k_ref[i0 + 1], qTb[:, w:]) + bias1, NEG_INF)
    m1 = jnp.max(s_a1, axis=0, keepdims=True)
    m2 = jnp.maximum(jnp.max(s_a2, axis=0, keepdims=True), jnp.max(s_b2, axis=0, keepdims=True))
    m = jnp.concatenate([m1, m2], axis=1)
    p_a = jnp.concatenate([jnp.exp2(s_a1 - m1), jnp.exp2(s_a2 - m2)], axis=1).astype(BF16)
    acc = _mm(vT_ref[i0], p_a)
    acc_b2 = _mm(vT_ref[i0 + 1], jnp.exp2(s_b2 - m2).astype(BF16))
    acc = jnp.concatenate([acc[:, :w], acc[:, w:] + acc_b2], axis=1)

    score_stage(0, sa_ref, ta_ref)
    n_steps = (i0 + (ATTN_TILES - 1)) // ATTN_TILES

    def body(pair, carry):
        step = 2 * pair
        carry = softmax_stage(step, sa_ref, ta_ref, *carry)
        score_stage(step + 1, sb_ref, tb_ref)

        def second_step(c):
            c = softmax_stage(step + 1, sb_ref, tb_ref, *c)
            score_stage(step + 2, sa_ref, ta_ref)
            return c

        return lax.cond(step + 1 < n_steps, second_step, lambda c: c, carry)

    m, acc = lax.fori_loop(0, (n_steps + 1) // 2, body, (m, acc))
    out = acc[:HEAD_DIM] / acc[HEAD_DIM:HEAD_DIM + 1]
    o_ref[...] = out.T.astype(o_ref.dtype)


def _moba_prompt(qT, kb, vT, kbar, slopes, batch, n_blk):
    d, m = qT.shape
    assert n_blk % ATTN_QBLOCKS == 0 and n_blk % ATTN_TILES == 0
    n_grp = n_blk // ATTN_QBLOCKS
    return pl.pallas_call(
        _attn_kernel,
        grid=(batch, N_HEADS, n_grp),
        in_specs=[pl.BlockSpec(memory_space=pltpu.SMEM),
                  pl.BlockSpec((HEAD_DIM, QW), lambda b, h, g: (h, b * n_grp + g)),
                  pl.BlockSpec((n_blk, MOBA_BLOCK, HEAD_DIM), lambda b, h, g: (b, 0, h)),
                  pl.BlockSpec((n_blk, VT_ROWS, MOBA_BLOCK), lambda b, h, g: (b, h, 0)),
                  pl.BlockSpec((1, n_blk, HEAD_DIM), lambda b, h, g: (b, 0, h))],
        out_specs=pl.BlockSpec((QW, HEAD_DIM), lambda b, h, g: (b * n_grp + g, h)),
        out_shape=jax.ShapeDtypeStruct((m, d), BF16),
        scratch_shapes=[pltpu.VMEM((ATTN_TILES, MOBA_BLOCK, QW), F32),
                        pltpu.VMEM((ATTN_TILES, 1, QW), F32)] * 2,
        compiler_params=_params("parallel", "parallel", "parallel"),
        name="moba_prompt",
    )(slopes, qT, kb, vT, kbar)


CONV_ROW_CHUNK = 32


def _dwconv_kernel(cur_ref, prev_ref, w_ref, b_ref, o_ref, ext_ref, shift_ref, *, tiles_per_seq):
    i = pl.program_id(0)
    tm, tc = cur_ref.shape
    starts_sequence = (i % tiles_per_seq) == 0
    ext_ref[0:CONV_HALO, :] = jnp.where(starts_sequence, 0.0, prev_ref[...])
    ext_ref[CONV_HALO:, :] = cur_ref[...]
    n_shifted = tm + CONV_HALO - SUBLANES
    for phase in range(1, SUBLANES):
        shift_ref[phase - 1, 0:n_shifted, :] = ext_ref[phase:phase + n_shifted, :]
    first = CONV_HALO - (CONV_WIDTH - 1)
    for c0 in range(0, tc, LANES):
        for r0 in range(0, tm, CONV_ROW_CHUNK):
            acc = jnp.broadcast_to(b_ref[:, c0:c0 + LANES], (CONV_ROW_CHUNK, LANES))
            for k in range(CONV_WIDTH):
                aligned, phase = divmod(first + k, SUBLANES)
                src = ext_ref if phase == 0 else shift_ref.at[phase - 1]
                row = r0 + aligned * SUBLANES
                acc = acc + w_ref[k:k + 1, c0:c0 + LANES] * src[row:row + CONV_ROW_CHUNK, c0:c0 + LANES]
            o_ref[r0:r0 + CONV_ROW_CHUNK, c0:c0 + LANES] = acc


def _dwconv_prompt(u, w_dw, b_dw, seq, tm, tc):
    m, c = u.shape
    halo_per_tile = tm // CONV_HALO
    return pl.pallas_call(
        functools.partial(_dwconv_kernel, tiles_per_seq=seq // tm),
        grid=(m // tm, c // tc),
        in_specs=[pl.BlockSpec((tm, tc), lambda i, j: (i, j)),
                  pl.BlockSpec((CONV_HALO, tc), lambda i, j: (jnp.maximum(i * halo_per_tile - 1, 0), j)),
                  pl.BlockSpec((CONV_WIDTH, tc), lambda i, j: (0, j)),
                  pl.BlockSpec((1, tc), lambda i, j: (0, j))],
        out_specs=pl.BlockSpec((tm, tc), lambda i, j: (i, j)),
        out_shape=jax.ShapeDtypeStruct((m, c), F32),
        scratch_shapes=[pltpu.VMEM((tm + CONV_HALO, tc), F32),
                        pltpu.VMEM((SUBLANES - 1, tm + CONV_HALO - SUBLANES, tc), F32)],
        compiler_params=_params("parallel", "parallel"),
        name="dwconv_prompt",
    )(u, u, w_dw, b_dw)


def _dwconv_sample_kernel(u_ref, w_ref, b_ref, o_ref):
    for t in range(o_ref.shape[0]):
        acc = jnp.broadcast_to(b_ref[...], o_ref.shape[1:])
        for k in range(CONV_WIDTH):
            acc = acc + w_ref[k:k + 1, :] * u_ref[t + k]
        o_ref[t] = acc


def _dwconv_sample(u_tmajor, w_dw, b_dw, n_new, tc):
    t_ext, nb, c = u_tmajor.shape
    return pl.pallas_call(
        _dwconv_sample_kernel,
        grid=(c // tc,),
        in_specs=[pl.BlockSpec((t_ext, nb, tc), lambda j: (0, 0, j)),
                  pl.BlockSpec((CONV_WIDTH, tc), lambda j: (0, j)),
                  pl.BlockSpec((1, tc), lambda j: (0, j))],
        out_specs=pl.BlockSpec((n_new, nb, tc), lambda j: (0, 0, j)),
        out_shape=jax.ShapeDtypeStruct((n_new, nb, c), F32),
        compiler_params=_params("parallel"),
        name="dwconv_sample",
    )(u_tmajor, w_dw, b_dw)


def _ln_silu_mm_kernel(x_ref, g_ref, b_ref, w_ref, o_ref):
    y = _layer_norm(x_ref[...], g_ref[...], b_ref[...])
    hidden = y * _sigmoid(y)
    o_ref[...] = _mm(hidden.astype(BF16), w_ref[...])


def _resident(shape):
    return pl.BlockSpec(shape, lambda *_: (0,) * len(shape), pipeline_mode=pl.Buffered(1))


def _ln_silu_mm(x, g, b, w, tm):
    m, c = x.shape
    n = w.shape[1]
    return pl.pallas_call(
        _ln_silu_mm_kernel,
        grid=(m // tm,),
        in_specs=[pl.BlockSpec((tm, c), lambda i: (i, 0)),
                  _resident((1, c)), _resident((1, c)), _resident((c, n))],
        out_specs=pl.BlockSpec((tm, n), lambda i: (i, 0)),
        out_shape=jax.ShapeDtypeStruct((m, n), F32),
        compiler_params=_params("parallel"),
        name="ln_silu_mm",
    )(x, g, b, w)


def _merge_kernel(oa_ref, ga_ref, gb_ref, oc_ref, x_ref, wa_ref, wo_ref, g_ref, b_ref, h_ref):
    br_attn = _mm(oa_ref[...], wa_ref[...])
    mixed = ga_ref[...] * br_attn + gb_ref[...] * oc_ref[...]
    y = DEEPNORM_ALPHA * x_ref[...] + _mm(mixed.astype(BF16), wo_ref[...])
    h_ref[...] = _layer_norm(y, g_ref[...], b_ref[...])


def _merge(o_attn, gates, o_conv, x, w_attn_o, w_out, g, b, tm):
    m, d = x.shape
    row = lambda i: (i, 0)
    return pl.pallas_call(
        _merge_kernel,
        grid=(m // tm,),
        in_specs=[pl.BlockSpec((tm, d), row),
                  pl.BlockSpec((tm, d), lambda i: (i, 0)),
                  pl.BlockSpec((tm, d), lambda i: (i, 1)),
                  pl.BlockSpec((tm, d), row),
                  pl.BlockSpec((tm, d), row),
                  _resident((d, d)), _resident((d, d)), _resident((1, d)), _resident((1, d))],
        out_specs=pl.BlockSpec((tm, d), row),
        out_shape=jax.ShapeDtypeStruct((m, d), F32),
        compiler_params=_params("parallel"),
        name="merge",
    )(o_attn, gates, gates, o_conv, x, w_attn_o, w_out, g, b)


PAGES_PER_STEP = 8


def _block_mean_body(page_refs, o_ref):
    for blk in range(PAGES_PER_STEP // PAGES_PER_BLOCK):
        total = None
        for p in range(PAGES_PER_BLOCK):
            part = jnp.sum(page_refs[blk * PAGES_PER_BLOCK + p][0, 0], axis=0)
            total = part if total is None else total + part
        o_ref[0, blk] = total * (1.0 / MOBA_BLOCK)


def _mlp_means_kernel(pt_ref, h_ref, w1_ref, w2_ref, g_ref, b_ref, *refs):
    page_refs = refs[:PAGES_PER_STEP]
    o_ref, kbar_ref, hb_ref, acc_ref = refs[PAGES_PER_STEP:]
    _mlp_kernel(h_ref, w1_ref, w2_ref, g_ref, b_ref, o_ref, hb_ref, acc_ref,
                side_work=functools.partial(_block_mean_body, page_refs, kbar_ref))


def _mlp_kernel(h_ref, w1_ref, w2_ref, g_ref, b_ref, o_ref, hb_ref, acc_ref, side_work=None):
    f = pl.program_id(1)

    @pl.when(f == 0)
    def _():
        hb_ref[...] = h_ref[...].astype(BF16)
        acc_ref[...] = jnp.zeros_like(acc_ref)

    a = _mm(hb_ref[...], w1_ref[...])
    a = jnp.square(jnp.maximum(a, 0.0))
    if side_work is not None:
        side_work()
    acc_ref[...] += _mm(a.astype(BF16), w2_ref[...])

    @pl.when(f == pl.num_programs(1) - 1)
    def _():
        y = DEEPNORM_ALPHA * h_ref[...] + acc_ref[...]
        o_ref[...] = _layer_norm(y, g_ref[...], b_ref[...])


def _mlp(h, w1, w2, g, b, tm, tf):
    m, d = h.shape
    ff = w1.shape[1]
    return pl.pallas_call(
        _mlp_kernel,
        grid=(m // tm, ff // tf),
        in_specs=[pl.BlockSpec((tm, d), lambda i, f: (i, 0)),
                  pl.BlockSpec((d, tf), lambda i, f: (0, f)),
                  pl.BlockSpec((tf, d), lambda i, f: (f, 0)),
                  pl.BlockSpec((1, d), lambda i, f: (0, 0)),
                  pl.BlockSpec((1, d), lambda i, f: (0, 0))],
        out_specs=pl.BlockSpec((tm, d), lambda i, f: (i, 0)),
        out_shape=jax.ShapeDtypeStruct((m, d), F32),
        scratch_shapes=[pltpu.VMEM((tm, d), BF16), pltpu.VMEM((tm, d), F32)],
        compiler_params=_params("parallel", "arbitrary"),
        name="mlp",
    )(h, w1, w2, g, b)


def _mlp_with_block_means(h, w1, w2, g, b, tm, tf, cache, page_table_flat):
    m, d = h.shape
    ff = w1.shape[1]
    n_f = ff // tf
    n_groups = page_table_flat.shape[0] // PAGES_PER_STEP
    assert (m // tm) * n_f >= n_groups
    blocks_per_step = PAGES_PER_STEP // PAGES_PER_BLOCK

    def group(i, f):
        return jnp.minimum(i * n_f + f, n_groups - 1)

    def page_spec(p):
        return pl.BlockSpec((1, 1, PAGE_SIZE, N_HEADS, HEAD_DIM),
                            lambda i, f, pt: (0, pt[group(i, f) * PAGES_PER_STEP + p], 0, 0, 0))

    y, means = pl.pallas_call(
        _mlp_means_kernel,
        grid_spec=pltpu.PrefetchScalarGridSpec(
            num_scalar_prefetch=1,
            grid=(m // tm, n_f),
            in_specs=[pl.BlockSpec((tm, d), lambda i, f, pt: (i, 0)),
                      pl.BlockSpec((d, tf), lambda i, f, pt: (0, f)),
                      pl.BlockSpec((tf, d), lambda i, f, pt: (f, 0)),
                      pl.BlockSpec((1, d), lambda i, f, pt: (0, 0)),
                      pl.BlockSpec((1, d), lambda i, f, pt: (0, 0))]
                     + [page_spec(p) for p in range(PAGES_PER_STEP)],
            out_specs=[pl.BlockSpec((tm, d), lambda i, f, pt: (i, 0)),
                       pl.BlockSpec((1, blocks_per_step, N_HEADS, HEAD_DIM),
                                    lambda i, f, pt: (group(i, f), 0, 0, 0))],
            scratch_shapes=[pltpu.VMEM((tm, d), BF16), pltpu.VMEM((tm, d), F32)]),
        out_shape=[jax.ShapeDtypeStruct((m, d), F32),
                   jax.ShapeDtypeStruct((n_groups, blocks_per_step, N_HEADS, HEAD_DIM), F32)],
        compiler_params=_params("arbitrary", "arbitrary"),
        name="mlp_block_means",
    )(page_table_flat, h, w1, w2, g, b, *([cache] * PAGES_PER_STEP))
    return y, means


def _block_mean_kernel(pt_ref, *refs):
    _block_mean_body(refs[:PAGES_PER_STEP], refs[PAGES_PER_STEP])


def _block_means(cache, page_table_flat):
    n_groups = page_table_flat.shape[0] // PAGES_PER_STEP
    blocks_per_step = PAGES_PER_STEP // PAGES_PER_BLOCK

    def page_spec(p):
        return pl.BlockSpec((1, 1, PAGE_SIZE, N_HEADS, HEAD_DIM),
                            lambda s, pt: (0, pt[s * PAGES_PER_STEP + p], 0, 0, 0))

    return pl.pallas_call(
        _block_mean_kernel,
        grid_spec=pltpu.PrefetchScalarGridSpec(
            num_scalar_prefetch=1,
            grid=(n_groups,),
            in_specs=[page_spec(p) for p in range(PAGES_PER_STEP)],
            out_specs=pl.BlockSpec((1, blocks_per_step, N_HEADS, HEAD_DIM), lambda s, pt: (s, 0, 0, 0))),
        out_shape=jax.ShapeDtypeStruct((n_groups, blocks_per_step, N_HEADS, HEAD_DIM), F32),
        compiler_params=_params("parallel"),
        name="block_means",
    )(page_table_flat, *([cache] * PAGES_PER_STEP))


def _sample_gate_kernel(q_ref, kbar_ref, o_ref):
    n_new = q_ref.shape[1]
    n_blk = kbar_ref.shape[1]
    kbar = kbar_ref[0]
    blk_row = lax.broadcasted_iota(jnp.int32, (n_blk, N_HEADS), 0)
    head_col = lax.broadcasted_iota(jnp.int32, (n_blk, N_HEADS), 1)
    for t in range(n_new):
        prod = kbar * q_ref[0, t:t + 1, :]
        scores = jnp.zeros((n_blk, N_HEADS), F32)
        for h in range(N_HEADS):
            head_score = jnp.sum(prod[:, h * HEAD_DIM:(h + 1) * HEAD_DIM], axis=1, keepdims=True)
            scores = jnp.where(head_col == h, head_score, scores)
        picks = _top3_blocks(scores, blk_row)
        for r in range(MOBA_TOPK):
            o_ref[0, t * MOBA_TOPK + r:t * MOBA_TOPK + r + 1, :] = picks[r]


def _sample_gate(q, kbar):
    nb, n_new, d = q.shape
    n_blk = kbar.shape[1]
    return pl.pallas_call(
        _sample_gate_kernel,
        grid=(nb,),
        in_specs=[pl.BlockSpec((1, n_new, d), lambda b: (b, 0, 0)),
                  pl.BlockSpec((1, n_blk, d), lambda b: (b, 0, 0))],
        out_specs=pl.BlockSpec((1, n_new * MOBA_TOPK, N_HEADS), lambda b: (b, 0, 0)),
        out_shape=jax.ShapeDtypeStruct((nb, n_new * MOBA_TOPK, N_HEADS), jnp.int32),
        compiler_params=_params("parallel"),
        name="sample_gate",
    )(q, kbar)


N_SEL_PAGES = MOBA_TOPK * PAGES_PER_BLOCK


def _page_tile(page_ref):
    return page_ref.reshape(PAGE_SIZE, HEAD_DIM)[...].astype(BF16)


def _sample_attn_kernel(sel_ref, pt_ref, slopes_ref, q_ref, ko_ref, vo_ref, *refs, past_len):
    n_new = q_ref.shape[1]
    k_pages = refs[:n_new * N_SEL_PAGES]
    v_pages = refs[n_new * N_SEL_PAGES:2 * n_new * N_SEL_PAGES]
    o_ref = refs[-1]
    b, h = pl.program_id(0), pl.program_id(1)
    n_keys = N_SEL_PAGES * PAGE_SIZE
    slope = jnp.full((1, n_keys), slopes_ref[h], F32)
    lane = lax.broadcasted_iota(jnp.int32, (1, PAGE_SIZE), 1)
    row = lax.broadcasted_iota(jnp.int32, (n_new, 1), 0)

    def stacked(pages, t):
        return jnp.concatenate([_page_tile(pages[t * N_SEL_PAGES + x]) for x in range(N_SEL_PAGES)], axis=0)

    logits, own_logits = [], []
    for t in range(n_new):
        q = q_ref[0, t:t + 1, :]
        q8 = jnp.broadcast_to(q, (8, HEAD_DIM)).astype(BF16)
        sel_base = ((b * N_HEADS + h) * n_new + t) * MOBA_TOPK
        dist = jnp.concatenate(
            [(past_len + t - sel_ref[sel_base + r] * MOBA_BLOCK - p * PAGE_SIZE) - lane
             for r in range(MOBA_TOPK) for p in range(PAGES_PER_BLOCK)], axis=1)
        s = lax.dot_general(q8, stacked(k_pages, t), (((1,), (1,)), ((), ())), preferred_element_type=F32)[0:1]
        logits.append(s * ATTN_SCALE - slope * dist.astype(F32))
        s_own = jnp.sum(ko_ref[0] * q, axis=1, keepdims=True)
        l_own = s_own * ATTN_SCALE - slope[:, 0:1] * (t - row).astype(F32)
        own_logits.append(jnp.where(row <= t, l_own, NEG_INF))

    weights = []
    for lg, l_own in zip(logits, own_logits):
        m = jnp.maximum(jnp.max(l_own, axis=0, keepdims=True), jnp.max(lg, axis=1, keepdims=True))
        p_own = jnp.exp(l_own - m)
        p = jnp.exp(lg - m)
        denom = jnp.sum(p_own, axis=0, keepdims=True) + jnp.sum(p, axis=1, keepdims=True)
        weights.append((p_own, p, denom))

    for t, (p_own, p, denom) in enumerate(weights):
        out = jnp.sum(p_own * vo_ref[0], axis=0, keepdims=True)
        out = out + _mm(jnp.broadcast_to(p, (8, n_keys)).astype(BF16), stacked(v_pages, t))[0:1]
        o_ref[0, t:t + 1, :] = out / denom


def _sample_attn(sel_flat, page_table_flat, slopes, q, k_new, v_new, cache_k, cache_v, n_pages, past_len):
    nb, n_new, d = q.shape

    def page_spec(t, r, p):
        def index(b, h, sel, pt):
            blk = sel[((b * N_HEADS + h) * n_new + t) * MOBA_TOPK + r]
            return (pt[b * n_pages + blk * PAGES_PER_BLOCK + p], 0, h, 0, 0)
        return pl.BlockSpec((1, PAGE_SIZE, 1, 1, HEAD_DIM), index)

    page_specs = [page_spec(t, r, p) for t in range(n_new) for r in range(MOBA_TOPK)
                  for p in range(PAGES_PER_BLOCK)]
    head_spec = pl.BlockSpec((1, n_new, HEAD_DIM), lambda b, h, sel, pt: (b, 0, h))
    return pl.pallas_call(
        functools.partial(_sample_attn_kernel, past_len=past_len),
        grid_spec=pltpu.PrefetchScalarGridSpec(
            num_scalar_prefetch=2,
            grid=(nb, N_HEADS),
            in_specs=[pl.BlockSpec(memory_space=pltpu.SMEM), head_spec, head_spec, head_spec]
                     + page_specs + page_specs,
            out_specs=head_spec),
        out_shape=jax.ShapeDtypeStruct((nb, n_new, d), F32),
        compiler_params=_params("parallel", "parallel"),
        name="sample_attn",
    )(sel_flat, page_table_flat, slopes, q, k_new, v_new,
      *([cache_k] * len(page_specs)), *([cache_v] * len(page_specs)))


def _alibi_slopes():
    return jnp.asarray(2.0 ** (-8.0 * np.arange(1, N_HEADS + 1) / N_HEADS), F32)


def kernel(x_prompt, x_sample, cache_k, cache_v, state_conv, page_table, w_in, b_in, w_attn_o,
           w_dw, b_dw, ln_conv_g, ln_conv_b, w_conv_o, w_out, ln1_g, ln1_b, w_mlp1, w_mlp2,
           ln2_g, ln2_b):
    batch, seq, d = x_prompt.shape
    dec_batch, dec_seq, _ = x_sample.shape
    n_pages = page_table.shape[1]
    past_len = n_pages * PAGE_SIZE
    n_blk = seq // MOBA_BLOCK
    assert w_in.shape[0] == DEPTH and d == N_HEADS * HEAD_DIM
    assert seq % MOBA_BLOCK == 0 and past_len % MOBA_BLOCK == 0 and n_pages % PAGES_PER_STEP == 0

    row2 = lambda a: a[0].reshape(1, -1)
    w_in_b = w_in[0].astype(BF16)
    b_in2 = row2(b_in)
    w_attn_o_b, w_conv_o_b, w_out_b = w_attn_o[0].astype(BF16), w_conv_o[0].astype(BF16), w_out[0].astype(BF16)
    w_mlp1_b, w_mlp2_b = w_mlp1[0].astype(BF16), w_mlp2[0].astype(BF16)
    w_dw2, b_dw2 = w_dw[0], row2(b_dw)
    slopes = _alibi_slopes()
    c_q, c_k, c_v, c_a, c_g, c_gate = 0, d, 2 * d, 3 * d, 4 * d, 5 * d

    mlp_args = (w_mlp1_b, w_mlp2_b, row2(ln2_g), row2(ln2_b))
    tf = 512

    def merged(x, o_attn, gates, o_conv, tm_merge):
        return _merge(o_attn, gates, o_conv, x, w_attn_o_b, w_out_b, row2(ln1_g), row2(ln1_b), tm_merge)

    m_p = batch * seq
    xp = x_prompt.reshape(m_p, d)
    xp_b = xp.astype(BF16)
    tm, tn = 1024, 512
    qT = _proj_qT(xp_b, w_in_b, b_in2, c_q, d, tm, tn)
    k_f, k_b, kbar = _proj_k(xp_b, w_in_b, b_in2, c_k, d, tm, tn)
    v_f, vT = _proj_v(xp_b, w_in_b, b_in2, c_v, d, tm, tn)
    glu = _proj_glu(xp_b, w_in_b, b_in2, c_a, c_g, d, tm, tn)
    gates = _proj_plain(xp_b, w_in_b, b_in2, c_gate, 2 * d, tm, tn, sigmoid=True)

    o_attn = _moba_prompt(qT, k_b.reshape(m_p // MOBA_BLOCK, MOBA_BLOCK, d), vT,
                          kbar.reshape(batch, n_blk, d), slopes, batch, n_blk)
    conv = _dwconv_prompt(glu, w_dw2, b_dw2, seq, 256, 512)
    o_conv = _ln_silu_mm(conv, row2(ln_conv_g), row2(ln_conv_b), w_conv_o_b, 256)
    h_prompt = merged(xp, o_attn, gates, o_conv, 256)
    pt_flat = page_table.reshape(-1)
    tm_mlp = 512
    n_groups = pt_flat.shape[0] // PAGES_PER_STEP
    if (m_p // tm_mlp) * (w_mlp1_b.shape[1] // tf) >= n_groups:
        y_prompt, means = _mlp_with_block_means(h_prompt, *mlp_args, tm_mlp, tf, cache_k, pt_flat)
    else:
        y_prompt = _mlp(h_prompt, *mlp_args, tm_mlp, tf)
        means = _block_means(cache_k, pt_flat)
    kbar_s = means.reshape(dec_batch, past_len // MOBA_BLOCK, d)

    new_k_prompt = k_f.reshape(DEPTH, batch, seq // PAGE_SIZE, PAGE_SIZE, N_HEADS, HEAD_DIM)
    new_v_prompt = v_f.reshape(DEPTH, batch, seq // PAGE_SIZE, PAGE_SIZE, N_HEADS, HEAD_DIM)
    new_conv_prompt = glu.reshape(batch, seq, d)[:, seq - (CONV_WIDTH - 1):][None]

    m_s = dec_batch * dec_seq
    xs = x_sample.reshape(m_s, d)
    xs_b = xs.astype(BF16)
    qkv_s = _proj_plain(xs_b, w_in_b, b_in2, c_q, 3 * d, m_s, tn)
    q_s = qkv_s[:, :d].reshape(dec_batch, dec_seq, d)
    k_s = qkv_s[:, d:2 * d].reshape(dec_batch, dec_seq, d)
    v_s = qkv_s[:, 2 * d:].reshape(dec_batch, dec_seq, d)
    glu_s = _proj_glu(xs_b, w_in_b, b_in2, c_a, c_g, d, m_s, tn)
    gates_s = _proj_plain(xs_b, w_in_b, b_in2, c_gate, 2 * d, m_s, tn, sigmoid=True)

    ck = cache_k[0].reshape(cache_k.shape[1], PAGE_SIZE, N_HEADS, 1, HEAD_DIM)
    cv = cache_v[0].reshape(cache_v.shape[1], PAGE_SIZE, N_HEADS, 1, HEAD_DIM)
    sel = _sample_gate(q_s, kbar_s)
    sel_flat = sel.reshape(dec_batch, dec_seq, MOBA_TOPK, N_HEADS).transpose(0, 3, 1, 2).reshape(-1)
    o_attn_s = _sample_attn(sel_flat, pt_flat, slopes, q_s, k_s, v_s, ck, cv, n_pages, past_len)

    u_ext = jnp.concatenate([state_conv[0], glu_s.reshape(dec_batch, dec_seq, d)], axis=1)
    conv_s = _dwconv_sample(u_ext.transpose(1, 0, 2), w_dw2, b_dw2, dec_seq, 512)
    conv_s = conv_s.transpose(1, 0, 2).reshape(m_s, d)
    o_conv_s = _ln_silu_mm(conv_s, row2(ln_conv_g), row2(ln_conv_b), w_conv_o_b, m_s)
    h_sample = merged(xs, o_attn_s.reshape(m_s, d).astype(BF16), gates_s, o_conv_s, m_s)
    y_sample = _mlp(h_sample, *mlp_args, m_s, tf)

    new_k_sample = k_s.reshape(DEPTH, dec_batch, dec_seq, N_HEADS, HEAD_DIM)
    new_v_sample = v_s.reshape(DEPTH, dec_batch, dec_seq, N_HEADS, HEAD_DIM)
    new_conv_sample = u_ext[:, dec_seq:][None]

    return (y_prompt.reshape(batch, seq, d), y_sample.reshape(dec_batch, dec_seq, d),
            new_k_prompt, new_v_prompt, new_conv_prompt, new_k_sample, new_v_sample, new_conv_sample)
```
